```python
import math
import jax, jax.numpy as jnp
from jax import lax
import numpy as np

D_MODEL = 1024
BATCH = 32
SEQ = 2048
DEPTH = 4

GRID_W = 64
CTX_LEN = 256
N_MIXERS = 2
N_RET = (DEPTH + N_MIXERS - 1) // N_MIXERS
N_POOL = DEPTH // N_MIXERS
RET_HEADS = 4
RET_QK_DIM = D_MODEL // RET_HEADS
RET_V_DIM = 2 * D_MODEL // RET_HEADS
RET_QK_TOTAL = RET_HEADS * RET_QK_DIM
RET_V_TOTAL = RET_HEADS * RET_V_DIM
RET_CHUNK = 128
POOL_WINDOWS = (2, 4, 8, 16)
POOL_GROUP = D_MODEL // len(POOL_WINDOWS)
FFN_HIDDEN = ((8 * D_MODEL + 3 * 256 - 1) // (3 * 256)) * 256
ROPE_THETA = 10000.0
EPS = 1e-6
GN_EPS = 1e-5

kernel_name = "hybrid_retention_pool_dit_trunk"


def _rmsnorm(x, w):
    x32 = x.astype(jnp.float32)
    y = x32 * lax.rsqrt(jnp.mean(x32 * x32, axis=-1, keepdims=True) + EPS)
    return (y * w.astype(jnp.float32)).astype(x.dtype)


def _modulate(h, shift, scale):
    return h * (1.0 + scale) + shift


def _rope(x, pos):
    d = x.shape[-1]
    inv = ROPE_THETA ** (-jnp.arange(0, d, 2, dtype=jnp.float32) / d)
    ang = pos[:, None] * inv[None, :]
    cos = jnp.cos(ang)[None, :, None, :].astype(x.dtype)
    sin = jnp.sin(ang)[None, :, None, :].astype(x.dtype)
    x1, x2 = x[..., : d // 2], x[..., d // 2:]
    return jnp.concatenate([x1 * cos - x2 * sin, x1 * sin + x2 * cos], axis=-1)


def _axial_rope(x, row, col):
    half = x.shape[-1] // 2
    return jnp.concatenate([_rope(x[..., :half], row), _rope(x[..., half:], col)], axis=-1)


def _retention_direction(q, k, v, log_g, s0, strict):
    bsz, length, heads, _ = q.shape
    dv = v.shape[-1]
    n_chunks = length // RET_CHUNK

    def to_chunks(t):
        return t.astype(jnp.float32).reshape(bsz, n_chunks, RET_CHUNK, heads, t.shape[-1]).transpose(1, 0, 3, 2, 4)

    qc, kc, vc = to_chunks(q), to_chunks(k), to_chunks(v)
    idx = jnp.arange(RET_CHUNK, dtype=jnp.float32)
    diff = idx[:, None] - idx[None, :]
    mask = (diff > 0) if strict else (diff >= 0)
    intra = jnp.where(mask, jnp.exp(log_g[:, None, None] * jnp.where(mask, diff, 0.0)), 0.0)
    q_decay = jnp.exp(log_g[:, None] * (idx + 1.0))[..., None]
    k_decay = jnp.exp(log_g[:, None] * (RET_CHUNK - 1.0 - idx))[..., None]
    chunk_decay = jnp.exp(log_g * RET_CHUNK)[:, None, None]

    def step(state, xs):
        qb, kb, vb = xs
        scores = jnp.einsum('bhid,bhjd->bhij', qb, kb) * intra
        out = (jnp.einsum('bhij,bhje->bhie', scores, vb)
               + jnp.einsum('bhid,bhde->bhie', qb * q_decay, state))
        state = state * chunk_decay + jnp.einsum('bhjd,bhje->bhde', kb * k_decay, vb)
        return state, out

    s_fin, out = lax.scan(step, s0.astype(jnp.float32), (qc, kc, vc))
    out = out.transpose(1, 0, 3, 2, 4).reshape(bsz, length, heads, dv)
    return out, s_fin


def _retention_output(y, g, gn_w, w_out, dtype):
    mu = jnp.mean(y, axis=-1, keepdims=True)
    var = jnp.mean(jnp.square(y - mu), axis=-1, keepdims=True)
    yn = ((y - mu) * lax.rsqrt(var + GN_EPS)).reshape(y.shape[0], y.shape[1], -1) * gn_w.astype(jnp.float32)
    return (jax.nn.silu(g.astype(jnp.float32)) * yn).astype(dtype) @ w_out


def _split_proj(p):
    bsz, length, _ = p.shape
    q = p[..., :RET_QK_TOTAL].reshape(bsz, length, RET_HEADS, RET_QK_DIM)
    k = p[..., RET_QK_TOTAL:2 * RET_QK_TOTAL].reshape(bsz, length, RET_HEADS, RET_QK_DIM)
    v = p[..., 2 * RET_QK_TOTAL:2 * RET_QK_TOTAL + RET_V_TOTAL].reshape(bsz, length, RET_HEADS, RET_V_DIM)
    g = p[..., 2 * RET_QK_TOTAL + RET_V_TOTAL:]
    return q, k, v, g


def _retention_mixer(h_lat, h_ctx, w_in, log_decay, gn_w, w_out, row, col, ctx_out):
    bsz = h_lat.shape[0]
    n_ctx = h_ctx.shape[1]
    lg = -jnp.abs(log_decay.astype(jnp.float32))
    q_scale = RET_QK_DIM ** -0.5
    zeros = jnp.zeros((bsz, RET_HEADS, RET_QK_DIM, RET_V_DIM), jnp.float32)
    flip = lambda t: jnp.flip(t, axis=1)

    if ctx_out:
        qc, kc, vc, gc = _split_proj(h_ctx @ w_in)
        qc = qc * q_scale
        yc_f, s_f = _retention_direction(qc, kc, vc, lg[0], zeros, False)
        yc_b, s_b = _retention_direction(flip(qc), flip(kc), flip(vc), lg[1], zeros, True)
        y_ctx = _retention_output(yc_f + flip(yc_b), gc, gn_w, w_out, h_ctx.dtype)
    else:
        p_kv = h_ctx @ w_in[:, RET_QK_TOTAL:2 * RET_QK_TOTAL + RET_V_TOTAL]
        kc = p_kv[..., :RET_QK_TOTAL].reshape(bsz, n_ctx, RET_HEADS, RET_QK_DIM).astype(jnp.float32)
        vc = p_kv[..., RET_QK_TOTAL:].reshape(bsz, n_ctx, RET_HEADS, RET_V_DIM).astype(jnp.float32)
        m = jnp.arange(n_ctx, dtype=jnp.float32)
        w_f = jnp.exp((n_ctx - 1.0 - m)[:, None] * lg[0][None, :])[None, :, :, None]
        w_b = jnp.exp(m[:, None] * lg[1][None, :])[None, :, :, None]
        s_f = jnp.einsum('blhd,blhe->bhde', kc * w_f, vc)
        s_b = jnp.einsum('blhd,blhe->bhde', kc * w_b, vc)
        y_ctx = None

    q, k, v, g = _split_proj(h_lat @ w_in)
    q = _axial_rope(q, row, col) * q_scale
    k = _axial_rope(k, row, col)
    y_f, _ = _retention_direction(q, k, v, lg[0], s_f, False)
    y_b, _ = _retention_direction(flip(q), flip(k), flip(v), lg[1], s_b, True)
    y_lat = _retention_output(y_f + flip(y_b), g, gn_w, w_out, h_lat.dtype)
    return y_lat, y_ctx


def _pool_mixer(h, pool_w, pool_scale):
    length = h.shape[1]
    hf = h.astype(jnp.float32)
    cs = jnp.concatenate([jnp.zeros_like(hf[:, :1]), jnp.cumsum(hf, axis=1)], axis=1)
    t = jnp.arange(length)
    outs = []
    for gi, w in enumerate(POOL_WINDOWS):
        sl = slice(gi * POOL_GROUP, (gi + 1) * POOL_GROUP)
        lo = jnp.clip(t - w // 2, 0, length)
        hi = jnp.clip(t + w // 2, 0, length)
        csg = cs[:, :, sl]
        mean = (csg[:, hi] - csg[:, lo]) / (hi - lo).astype(jnp.float32)[None, :, None]
        outs.append(jnp.einsum('bld,de->ble', (mean - hf[:, :, sl]).astype(h.dtype), pool_w[gi]))
    return jnp.concatenate(outs, axis=-1) * pool_scale


def _swiglu(h, w_in, w_out):
    a, b = jnp.split(h @ w_in, 2, axis=-1)
    return (jax.nn.silu(a) * b) @ w_out


def setup_inputs(seed: int = 0) -> dict:
    key = jax.random.key(seed)
    ks = jax.random.split(key, 18)
    nrm = jax.random.normal
    f32 = jnp.float32
    base_decay = np.log(1.0 - 2.0 ** (-5.0 - np.arange(RET_HEADS))).astype(np.float32)
    return {
        "x": nrm(ks[0], (BATCH, SEQ, D_MODEL), f32),
        "c": nrm(ks[1], (BATCH, D_MODEL), f32),
        "ctx": nrm(ks[2], (BATCH, CTX_LEN, D_MODEL), f32),
        "c_ctx": nrm(ks[3], (D_MODEL,), f32),
        "ada_w": nrm(ks[4], (DEPTH, D_MODEL, 6 * D_MODEL), f32) * (0.5 * D_MODEL ** -0.5),
        "ada_b": 0.01 * nrm(ks[5], (DEPTH, 6 * D_MODEL), f32),
        "norm1_w": 1.0 + 0.05 * nrm(ks[6], (DEPTH, D_MODEL), f32),
        "norm2_w": 1.0 + 0.05 * nrm(ks[7], (DEPTH, D_MODEL), f32),
        "ret_w_in": nrm(ks[8], (N_RET, D_MODEL, 2 * RET_QK_TOTAL + 2 * RET_V_TOTAL), f32) * D_MODEL ** -0.5,
        "ret_log_decay": jnp.asarray(base_decay)[None, None, :] * (1.0 + 0.1 * nrm(ks[9], (N_RET, 2, RET_HEADS), f32)),
        "ret_gn_w": 1.0 + 0.05 * nrm(ks[10], (N_RET, RET_V_TOTAL), f32),
        "ret_w_out": nrm(ks[11], (N_RET, RET_V_TOTAL, D_MODEL), f32) * RET_V_TOTAL ** -0.5,
        "pool_w": nrm(ks[12], (N_POOL, len(POOL_WINDOWS), POOL_GROUP, POOL_GROUP), f32) * POOL_GROUP ** -0.5,
        "pool_scale": 1.0 + 0.05 * nrm(ks[13], (N_POOL, D_MODEL), f32),
        "ffn_w_in": nrm(ks[14], (DEPTH, D_MODEL, 2 * FFN_HIDDEN), f32) * D_MODEL ** -0.5,
        "ffn_w_out": nrm(ks[15], (DEPTH, FFN_HIDDEN, D_MODEL), f32) * FFN_HIDDEN ** -0.5,
        "final_norm_w": 1.0 + 0.05 * nrm(ks[16], (D_MODEL,), f32),
    }


def reference(x, c, ctx, c_ctx, ada_w, ada_b, norm1_w, norm2_w, ret_w_in, ret_log_decay, ret_gn_w,
              ret_w_out, pool_w, pool_scale, ffn_w_in, ffn_w_out, final_norm_w):
    n_lat = x.shape[1]
    ROWS = n_lat // GRID_W
    row = jnp.repeat(jnp.arange(ROWS), GRID_W).astype(jnp.float32)
    col = jnp.tile(jnp.arange(GRID_W), ROWS).astype(jnp.float32)
    silu_c = jax.nn.silu(c)
    silu_cc = jax.nn.silu(c_ctx)
    x_lat, x_ctx = x, ctx
    for i in range(DEPTH):
        last = i == DEPTH - 1
        use_ret = (i % N_MIXERS) == 0
        j = i // N_MIXERS
        ctx_needed = (not last) or use_ret
        sh1, sc1, g1, sh2, sc2, g2 = jnp.split((silu_c @ ada_w[i] + ada_b[i])[:, None, :], 6, axis=-1)
        h_lat = _modulate(_rmsnorm(x_lat, norm1_w[i]), sh1, sc1)
        if ctx_needed:
            csh1, csc1, cg1, csh2, csc2, cg2 = jnp.split(silu_cc @ ada_w[i] + ada_b[i], 6, axis=-1)
            h_ctx = _modulate(_rmsnorm(x_ctx, norm1_w[i]), csh1, csc1)
        if use_ret:
            y_lat, y_ctx = _retention_mixer(h_lat, h_ctx, ret_w_in[j], ret_log_decay[j], ret_gn_w[j],
                                            ret_w_out[j], row, col, not last)
        else:
            y_lat = _pool_mixer(h_lat, pool_w[j], pool_scale[j])
            y_ctx = None if last else _pool_mixer(h_ctx, pool_w[j], pool_scale[j])
        x_lat = x_lat + g1 * y_lat
        x_lat = x_lat + g2 * _swiglu(_modulate(_rmsnorm(x_lat, norm2_w[i]), sh2, sc2), ffn_w_in[i], ffn_w_out[i])
        if not last:
            x_ctx = x_ctx + cg1 * y_ctx
            x_ctx = x_ctx + cg2 * _swiglu(_modulate(_rmsnorm(x_ctx, norm2_w[i]), csh2, csc2),
                                          ffn_w_in[i], ffn_w_out[i])
    return _rmsnorm(x_lat, final_norm_w)
```

```python
import functools

import jax
import jax.numpy as jnp
from jax import lax
from jax.experimental import pallas as pl
from jax.experimental.pallas import tpu as pltpu

D_MODEL = 1024
GRID_W = 64
N_MIXERS = 2
RET_HEADS = 4
RET_QK_DIM = D_MODEL // RET_HEADS
RET_V_DIM = 2 * D_MODEL // RET_HEADS
RET_QK_TOTAL = RET_HEADS * RET_QK_DIM
RET_V_TOTAL = RET_HEADS * RET_V_DIM
RET_IN_TOTAL = 2 * RET_QK_TOTAL + 2 * RET_V_TOTAL
POOL_WINDOWS = (2, 4, 8, 16)
POOL_GROUP = D_MODEL // len(POOL_WINDOWS)
POOL_HALO = 8
ROPE_THETA = 10000.0
EPS = 1e-6
GN_EPS = 1e-5
N_MOD = 6

F32 = jnp.float32
BF16 = jnp.bfloat16

VMEM_LIMIT_BYTES = 56 * 1024 * 1024
TOKEN_TILE = 256
RET_CHUNK = 256
MODS_COL_TILE = 1536


def _dot(a, b):
    return jnp.dot(a, b, preferred_element_type=F32)


def _rmsnorm(x, w):
    return x * lax.rsqrt(jnp.mean(x * x, axis=-1, keepdims=True) + EPS) * w


def _resident(shape):
    nd = len(shape)
    return pl.BlockSpec(shape, lambda *_: (0,) * nd, pipeline_mode=pl.Buffered(1))


def _mods_kernel(c_ref, w_ref, b_ref, o_ref):
    s = jax.nn.silu(c_ref[...]).astype(BF16)
    o_ref[...] = _dot(s, w_ref[...].astype(BF16)) + b_ref[...]


def _mods(c_all, ada_w, ada_b):
    depth, _, n = ada_w.shape
    r = c_all.shape[0]
    tn = MODS_COL_TILE
    return pl.pallas_call(
        _mods_kernel,
        grid=(depth, n // tn),
        in_specs=[
            pl.BlockSpec((r, D_MODEL), lambda l, j: (0, 0)),
            pl.BlockSpec((None, D_MODEL, tn), lambda l, j: (l, 0, j)),
            pl.BlockSpec((None, 1, tn), lambda l, j: (l, 0, j)),
        ],
        out_specs=pl.BlockSpec((None, r, tn), lambda l, j: (l, 0, j)),
        out_shape=jax.ShapeDtypeStruct((depth, r, n), F32),
        compiler_params=pltpu.CompilerParams(
            dimension_semantics=("arbitrary", "arbitrary"), vmem_limit_bytes=VMEM_LIMIT_BYTES),
        name="adaln_mods",
    )(c_all, ada_w, ada_b.reshape(depth, 1, n))


def _rope_half(t, cos, sin):
    return t * cos + pltpu.roll(t, 64, axis=1) * sin


def _ret_proj_kernel(*refs, rope):
    if rope:
        x_ref, mod_ref, nw_ref, w_ref, cos_ref, sin_ref, q_ref, k_ref, v_ref, g_ref = refs
    else:
        x_ref, mod_ref, nw_ref, w_ref, q_ref, k_ref, v_ref, g_ref = refs
    h = _rmsnorm(x_ref[...], nw_ref[...]) * (1.0 + mod_ref[1:2, :]) + mod_ref[0:1, :]
    hb = h.astype(BF16)
    q_scale = RET_QK_DIM ** -0.5
    half = RET_QK_DIM // 2
    for out_ref, base, scale in ((q_ref, 0, q_scale), (k_ref, RET_QK_TOTAL, None)):
        for hd in range(RET_HEADS):
            lo = hd * RET_QK_DIM
            t = _dot(hb, w_ref[:, base + lo:base + lo + RET_QK_DIM])
            if rope:
                t = jnp.concatenate(
                    [_rope_half(t[:, :half], cos_ref[:, :half], sin_ref[:, :half]),
                     _rope_half(t[:, half:], cos_ref[:, half:], sin_ref[:, half:])], axis=-1)
            if scale is not None:
                t = t * scale
            out_ref[:, lo:lo + RET_QK_DIM] = t.astype(BF16)
    v_lo = 2 * RET_QK_TOTAL
    v_ref[...] = _dot(hb, w_ref[:, v_lo:v_lo + RET_V_TOTAL]).astype(BF16)
    g_ref[...] = _dot(hb, w_ref[:, v_lo + RET_V_TOTAL:]).astype(BF16)


def _ret_proj(x, mod, nw, w_in, rope_tabs, seq_len):
    t_tokens = x.shape[0]
    tm = TOKEN_TILE
    tiles_per_seq = seq_len // tm
    per_batch = mod.shape[0] > 1
    mod_map = (lambda i: (i // tiles_per_seq, 0, 0)) if per_batch else (lambda i: (0, 0, 0))
    tok = lambda i: (i, 0)
    in_specs = [
        pl.BlockSpec((tm, D_MODEL), tok),
        pl.BlockSpec((None, N_MOD, D_MODEL), mod_map),
        _resident((1, D_MODEL)),
        _resident((D_MODEL, RET_IN_TOTAL)),
    ]
    args = [x, mod, nw.reshape(1, D_MODEL), w_in]
    rope = rope_tabs is not None
    if rope:
        pos = lambda i: (i % tiles_per_seq, 0)
        in_specs += [pl.BlockSpec((tm, RET_QK_DIM), pos), pl.BlockSpec((tm, RET_QK_DIM), pos)]
        args += list(rope_tabs)
    return pl.pallas_call(
        functools.partial(_ret_proj_kernel, rope=rope),
        grid=(t_tokens // tm,),
        in_specs=in_specs,
        out_specs=[
            pl.BlockSpec((tm, RET_QK_TOTAL), tok),
            pl.BlockSpec((tm, RET_QK_TOTAL), tok),
            pl.BlockSpec((tm, RET_V_TOTAL), tok),
            pl.BlockSpec((tm, RET_V_TOTAL), tok),
        ],
        out_shape=[
            jax.ShapeDtypeStruct((t_tokens, RET_QK_TOTAL), BF16),
            jax.ShapeDtypeStruct((t_tokens, RET_QK_TOTAL), BF16),
            jax.ShapeDtypeStruct((t_tokens, RET_V_TOTAL), BF16),
            jax.ShapeDtypeStruct((t_tokens, RET_V_TOTAL), BF16),
        ],
        compiler_params=pltpu.CompilerParams(
            dimension_semantics=("arbitrary",), vmem_limit_bytes=VMEM_LIMIT_BYTES),
        name="ret_proj",
    )(*args)


def _retention_kernel(lg_ref, gnw_ref, qc_ref, kc_ref, vc_ref, gc_ref, ql_ref, kl_ref, vl_ref, gl_ref,
                      zc_ref, zl_ref, y_scr, *, n_ctx, n_lat):
    c = RET_CHUNK
    hd = pl.program_id(1)
    lg_f = -jnp.abs(lg_ref[0, hd])
    lg_b = -jnp.abs(lg_ref[1, hd])
    ii = lax.broadcasted_iota(jnp.int32, (c, c), 0)
    jj = lax.broadcasted_iota(jnp.int32, (c, c), 1)
    diff = (ii - jj).astype(F32)
    causal = diff >= 0
    intra_f = jnp.where(causal, jnp.exp(lg_f * jnp.where(causal, diff, 0.0)), 0.0)
    intra_b = jnp.where(causal, 0.0, jnp.exp(lg_b * jnp.where(causal, 0.0, -diff)))
    idx = lax.broadcasted_iota(jnp.int32, (c, 1), 0).astype(F32)
    qd_f = jnp.exp(lg_f * (idx + 1.0))
    kd_f = jnp.exp(lg_f * (c - 1.0 - idx))
    qd_b = jnp.exp(lg_b * (c - idx))
    kd_b = jnp.exp(lg_b * idx)
    cd_f = jnp.exp(jnp.full((1, 1), lg_f * c, F32))
    cd_b = jnp.exp(jnp.full((1, 1), lg_b * c, F32))
    gnw = gnw_ref[...]

    def chunk_refs(kind, j):
        rows = pl.ds(j * c, c)
        if kind == "c":
            return qc_ref.at[rows], kc_ref.at[rows], vc_ref.at[rows], gc_ref.at[rows], zc_ref.at[rows]
        return ql_ref.at[rows], kl_ref.at[rows], vl_ref.at[rows], gl_ref.at[rows], zl_ref.at[rows]

    def step(q_r, k_r, v_r, state, intra, qd, kd, cd):
        q, k, v = q_r[...], k_r[...], v_r[...]
        scores = lax.dot_general(q, k, (((1,), (1,)), ((), ())), preferred_element_type=F32)
        p = (scores * intra).astype(BF16)
        out = _dot(p, v)
        kdt = (k.astype(F32) * kd).T.astype(BF16)
        upd = _dot(kdt, v)
        if state is None:
            return out, upd
        out = out + _dot((q.astype(F32) * qd).astype(BF16), state.astype(BF16))
        return out, state * cd + upd

    def finalize(y, g_r, z_r):
        mu = jnp.mean(y, axis=-1, keepdims=True)
        yc = y - mu
        var = jnp.mean(yc * yc, axis=-1, keepdims=True)
        yn = yc * lax.rsqrt(var + GN_EPS) * gnw
        z_r[...] = (jax.nn.silu(g_r[...].astype(F32)) * yn).astype(BF16)

    fwd_order = [("c", j) for j in range(n_ctx)] + [("l", j) for j in range(n_lat)]
    bwd_order = [("c", j) for j in reversed(range(n_ctx))] + [("l", j) for j in reversed(range(n_lat))]
    offset = {"c": 0, "l": n_ctx * c}
    pending = set()
    s_f = s_b = None
    for key_f, key_b in zip(fwd_order, bwd_order):
        q_r, k_r, v_r, g_f, z_f = chunk_refs(*key_f)
        out_f, s_f = step(q_r, k_r, v_r, s_f, intra_f, qd_f, kd_f, cd_f)
        q_r, k_r, v_r, g_b, z_b = chunk_refs(*key_b)
        out_b, s_b = step(q_r, k_r, v_r, s_b, intra_b, qd_b, kd_b, cd_b)
        if key_f == key_b:
            finalize(out_f + out_b, g_f, z_f)
            continue
        for key, out, g_r, z_r in ((key_f, out_f, g_f, z_f), (key_b, out_b, g_b, z_b)):
            rows = pl.ds(offset[key[0]] + key[1] * c, c)
            if key in pending:
                finalize(y_scr[rows, :] + out, g_r, z_r)
            else:
                y_scr[rows, :] = out
                pending.add(key)


def _retention(log_decay, gn_w, qc, kc, vc, gc, ql, kl, vl, gl):
    bsz, ctx_len, _ = qc.shape
    seq_len = ql.shape[1]
    n_ctx, n_lat = ctx_len // RET_CHUNK, seq_len // RET_CHUNK
    bh = lambda b, h: (b, 0, h)
    qk_c = pl.BlockSpec((None, ctx_len, RET_QK_DIM), bh)
    v_c = pl.BlockSpec((None, ctx_len, RET_V_DIM), bh)
    qk_l = pl.BlockSpec((None, seq_len, RET_QK_DIM), bh)
    v_l = pl.BlockSpec((None, seq_len, RET_V_DIM), bh)
    return pl.pallas_call(
        functools.partial(_retention_kernel, n_ctx=n_ctx, n_lat=n_lat),
        grid=(bsz, RET_HEADS),
        in_specs=[
            pl.BlockSpec(memory_space=pltpu.SMEM),
            pl.BlockSpec((1, RET_V_DIM), lambda b, h: (0, h)),
            qk_c, qk_c, v_c, v_c, qk_l, qk_l, v_l, v_l,
        ],
        out_specs=[v_c, v_l],
        out_shape=[
            jax.ShapeDtypeStruct((bsz, ctx_len, RET_V_TOTAL), BF16),
            jax.ShapeDtypeStruct((bsz, seq_len, RET_V_TOTAL), BF16),
        ],
        scratch_shapes=[pltpu.VMEM((ctx_len + seq_len, RET_V_DIM), F32)],
        compiler_params=pltpu.CompilerParams(
            dimension_semantics=("arbitrary", "arbitrary"), vmem_limit_bytes=VMEM_LIMIT_BYTES),
        name="retention",
    )(log_decay, gn_w.reshape(1, RET_V_TOTAL), qc, kc, vc, gc, ql, kl, vl, gl)


def _pool_kernel(x_ref, xp_ref, xn_ref, mod_ref, nw_ref, d_ref, h_scr, *, seq_len):
    tm = x_ref.shape[0]
    i = pl.program_id(1)
    nw = nw_ref[...]
    scale = 1.0 + mod_ref[1:2, :]
    shift = mod_ref[0:1, :]
    hal = POOL_HALO
    h = _rmsnorm(x_ref[...], nw) * scale + shift
    h_prev = _rmsnorm(xp_ref[...], nw) * scale + shift
    h_next = _rmsnorm(xn_ref[...], nw) * scale + shift
    h_scr[0:hal, :] = jnp.where(i > 0, h_prev, 0.0)
    h_scr[hal:hal + tm, :] = h
    h_scr[hal + tm:, :] = jnp.where(i < pl.num_programs(1) - 1, h_next, 0.0)
    t = i * tm + lax.broadcasted_iota(jnp.int32, (tm, 1), 0)
    for gi, w in enumerate(POOL_WINDOWS):
        cols = slice(gi * POOL_GROUP, (gi + 1) * POOL_GROUP)
        s = h_scr[hal - w // 2:hal - w // 2 + tm, cols]
        for off in range(-w // 2 + 1, w // 2):
            s = s + h_scr[hal + off:hal + off + tm, cols]
        cnt = jnp.minimum(t + w // 2, seq_len) - jnp.maximum(t - w // 2, 0)
        d_ref[:, cols] = (s / cnt.astype(F32) - h[:, cols]).astype(BF16)


def _pool(x, mod, nw, seq_len):
    t_tokens = x.shape[0]
    bsz = t_tokens // seq_len
    tm = min(TOKEN_TILE, seq_len)
    n_tiles = seq_len // tm
    halo_per_tile = tm // POOL_HALO
    n_halo = seq_len // POOL_HALO
    per_batch = mod.shape[0] > 1
    x3 = x.reshape(bsz, seq_len, D_MODEL)
    out = pl.pallas_call(
        functools.partial(_pool_kernel, seq_len=seq_len),
        grid=(bsz, n_tiles),
        in_specs=[
            pl.BlockSpec((None, tm, D_MODEL), lambda b, i: (b, i, 0)),
            pl.BlockSpec((None, POOL_HALO, D_MODEL),
                         lambda b, i: (b, jnp.maximum(i * halo_per_tile - 1, 0), 0)),
            pl.BlockSpec((None, POOL_HALO, D_MODEL),
                         lambda b, i: (b, jnp.minimum((i + 1) * halo_per_tile, n_halo - 1), 0)),
            pl.BlockSpec((None, N_MOD, D_MODEL),
                         (lambda b, i: (b, 0, 0)) if per_batch else (lambda b, i: (0, 0, 0))),
            _resident((1, D_MODEL)),
        ],
        out_specs=pl.BlockSpec((None, tm, D_MODEL), lambda b, i: (b, i, 0)),
        out_shape=jax.ShapeDtypeStruct((bsz, seq_len, D_MODEL), BF16),
        scratch_shapes=[pltpu.VMEM((tm + 2 * POOL_HALO, D_MODEL), F32)],
        compiler_params=pltpu.CompilerParams(
            dimension_semantics=("arbitrary", "arbitrary"), vmem_limit_bytes=VMEM_LIMIT_BYTES),
        name="pool_mix",
    )(x3, x3, x3, mod, nw.reshape(1, D_MODEL))
    return out.reshape(t_tokens, D_MODEL)


def _mix_ffn_kernel(*refs, pool, final):
    refs = list(refs)
    z_ref, x_ref, mod_ref, n2w_ref, wmix_ref = refs[:5]
    rest = refs[5:]
    ps_ref = rest.pop(0) if pool else None
    win_ref, wout_ref = rest.pop(0), rest.pop(0)
    fnw_ref = rest.pop(0) if final else None
    o_ref = rest.pop(0)

    z = z_ref[...]
    if pool:
        y = jnp.concatenate(
            [_dot(z[:, g * POOL_GROUP:(g + 1) * POOL_GROUP], wmix_ref[g]) for g in range(len(POOL_WINDOWS))],
            axis=-1) * ps_ref[...]
    else:
        y = _dot(z, wmix_ref[...])
    x1 = x_ref[...] + mod_ref[2:3, :] * y
    h2 = (_rmsnorm(x1, n2w_ref[...]) * (1.0 + mod_ref[4:5, :]) + mod_ref[3:4, :]).astype(BF16)
    ab = _dot(h2, win_ref[...])
    hidden = ab.shape[-1] // 2
    u = (jax.nn.silu(ab[:, :hidden]) * ab[:, hidden:]).astype(BF16)
    x2 = x1 + mod_ref[5:6, :] * _dot(u, wout_ref[...])
    if final:
        x2 = _rmsnorm(x2, fnw_ref[...])
    o_ref[...] = x2


def _mix_ffn(z, x, mod, n2w, w_mix, pool_scale, w_ffn_in, w_ffn_out, final_w, seq_len):
    t_tokens = x.shape[0]
    tm = TOKEN_TILE
    tiles_per_seq = seq_len // tm
    per_batch = mod.shape[0] > 1
    mod_map = (lambda i: (i // tiles_per_seq, 0, 0)) if per_batch else (lambda i: (0, 0, 0))
    tok = lambda i: (i, 0)
    pool = pool_scale is not None
    final = final_w is not None
    in_specs = [
        pl.BlockSpec((tm, z.shape[1]), tok),
        pl.BlockSpec((tm, D_MODEL), tok),
        pl.BlockSpec((None, N_MOD, D_MODEL), mod_map),
        _resident((1, D_MODEL)),
        _resident(w_mix.shape),
    ]
    args = [z, x, mod, n2w.reshape(1, D_MODEL), w_mix]
    if pool:
        in_specs.append(_resident((1, D_MODEL)))
        args.append(pool_scale.reshape(1, D_MODEL))
    in_specs += [_resident(w_ffn_in.shape), _resident(w_ffn_out.shape)]
    args += [w_ffn_in, w_ffn_out]
    if final:
        in_specs.append(_resident((1, D_MODEL)))
        args.append(final_w.reshape(1, D_MODEL))
    return pl.pallas_call(
        functools.partial(_mix_ffn_kernel, pool=pool, final=final),
        grid=(t_tokens // tm,),
        in_specs=in_specs,
        out_specs=pl.BlockSpec((tm, D_MODEL), tok),
        out_shape=jax.ShapeDtypeStruct((t_tokens, D_MODEL), F32),
        compiler_params=pltpu.CompilerParams(
            dimension_semantics=("arbitrary",), vmem_limit_bytes=VMEM_LIMIT_BYTES),
        name="mix_ffn_pool" if pool else "mix_ffn_ret",
    )(*args)


def _rope_tables(seq_len):
    rows = seq_len // GRID_W
    row = jnp.repeat(jnp.arange(rows), GRID_W).astype(F32)
    col = jnp.tile(jnp.arange(GRID_W), rows).astype(F32)
    d = RET_QK_DIM // 2
    inv = ROPE_THETA ** (-jnp.arange(0, d, 2, dtype=F32) / d)
    ang_r = row[:, None] * inv[None, :]
    ang_c = col[:, None] * inv[None, :]
    cos = jnp.concatenate([jnp.cos(ang_r), jnp.cos(ang_r), jnp.cos(ang_c), jnp.cos(ang_c)], axis=-1)
    sin = jnp.concatenate([-jnp.sin(ang_r), jnp.sin(ang_r), -jnp.sin(ang_c), jnp.sin(ang_c)], axis=-1)
    return cos, sin


def kernel(x, c, ctx, c_ctx, ada_w, ada_b, norm1_w, norm2_w, ret_w_in, ret_log_decay, ret_gn_w,
           ret_w_out, pool_w, pool_scale, ffn_w_in, ffn_w_out, final_norm_w):
    bsz, seq_len, _ = x.shape
    ctx_len = ctx.shape[1]
    depth = ada_w.shape[0]

    pad = (-(bsz + 1)) % 8
    c_all = jnp.concatenate([c, c_ctx[None, :], jnp.zeros((pad, D_MODEL), F32)], axis=0)
    mods = _mods(c_all, ada_w, ada_b).reshape(depth, bsz + 1 + pad, N_MOD, D_MODEL)

    rope_tabs = _rope_tables(seq_len)
    ret_w_in_b = ret_w_in.astype(BF16)
    ret_w_out_b = ret_w_out.astype(BF16)
    pool_w_b = pool_w.astype(BF16)
    ffn_w_in_b = ffn_w_in.astype(BF16)
    ffn_w_out_b = ffn_w_out.astype(BF16)

    x_lat = x.reshape(bsz * seq_len, D_MODEL)
    x_ctx = ctx.reshape(bsz * ctx_len, D_MODEL)
    for i in range(depth):
        last = i == depth - 1
        use_ret = (i % N_MIXERS) == 0
        j = i // N_MIXERS
        if last and use_ret:
            raise NotImplementedError("a final retention layer (context-state-only path) is not built")
        mod_lat = mods[i, :bsz]
        mod_ctx = mods[i, bsz:bsz + 1]
        final_w = final_norm_w if last else None
        if use_ret:
            ql, kl, vl, gl = _ret_proj(x_lat, mod_lat, norm1_w[i], ret_w_in_b[j], rope_tabs, seq_len)
            qc, kc, vc, gc = _ret_proj(x_ctx, mod_ctx, norm1_w[i], ret_w_in_b[j], None, ctx_len)
            r3 = lambda a, n: a.reshape(bsz, n, a.shape[-1])
            z_ctx, z_lat = _retention(
                ret_log_decay[j], ret_gn_w[j],
                r3(qc, ctx_len), r3(kc, ctx_len), r3(vc, ctx_len), r3(gc, ctx_len),
                r3(ql, seq_len), r3(kl, seq_len), r3(vl, seq_len), r3(gl, seq_len))
            z_lat = z_lat.reshape(bsz * seq_len, RET_V_TOTAL)
            z_ctx = z_ctx.reshape(bsz * ctx_len, RET_V_TOTAL)
            w_mix, p_scale = ret_w_out_b[j], None
        else:
            z_lat = _pool(x_lat, mod_lat, norm1_w[i], seq_len)
            z_ctx = None if last else _pool(x_ctx, mod_ctx, norm1_w[i], ctx_len)
            w_mix, p_scale = pool_w_b[j], pool_scale[j]
        x_lat = _mix_ffn(z_lat, x_lat, mod_lat, norm2_w[i], w_mix, p_scale,
                         ffn_w_in_b[i], ffn_w_out_b[i], final_w, seq_len)
        if not last:
            x_ctx = _mix_ffn(z_ctx, x_ctx, mod_ctx, norm2_w[i], w_mix, p_scale,
                             ffn_w_in_b[i], ffn_w_out_b[i], None, ctx_len)
    return x_lat.reshape(bsz, seq_len, D_MODEL)
```

```python
import functools

import jax
import jax.numpy as jnp
from jax import lax
from jax.experimental import pallas as pl
from jax.experimental.pallas import tpu as pltpu

D_MODEL = 1024
GRID_W = 64
N_MIXERS = 2
RET_HEADS = 4
RET_QK_DIM = D_MODEL // RET_HEADS
RET_V_DIM = 2 * D_MODEL // RET_HEADS
RET_QK_TOTAL = RET_HEADS * RET_QK_DIM
RET_V_TOTAL = RET_HEADS * RET_V_DIM
RET_IN_TOTAL = 2 * RET_QK_TOTAL + 2 * RET_V_TOTAL
POOL_WINDOWS = (2, 4, 8, 16)
POOL_GROUP = D_MODEL // len(POOL_WINDOWS)
POOL_HALO = 8
ROPE_THETA = 10000.0
EPS = 1e-6
GN_EPS = 1e-5
N_MOD = 6

F32 = jnp.float32
BF16 = jnp.bfloat16

VMEM_LIMIT_BYTES = 56 * 1024 * 1024
TOKEN_TILE = 512
SUB_TILE = 256
POOL_TILE = 256
RET_CHUNK = 256
MODS_COL_TILE = 1536


def _dot(a, b):
    return jnp.dot(a, b, preferred_element_type=F32)


def _rmsnorm(x, w):
    return x * lax.rsqrt(jnp.mean(x * x, axis=-1, keepdims=True) + EPS) * w


def _norm_mod(x, w, scale, shift):
    return _rmsnorm(x, w * (1.0 + scale)) + shift


def _resident(shape):
    nd = len(shape)
    return pl.BlockSpec(shape, lambda *_: (0,) * nd, pipeline_mode=pl.Buffered(1))


def _mods_kernel(c_ref, w_ref, b_ref, o_ref):
    s = jax.nn.silu(c_ref[...]).astype(BF16)
    o_ref[...] = _dot(s, w_ref[...].astype(BF16)) + b_ref[...]


def _mods(c_all, ada_w, ada_b):
    depth, _, n = ada_w.shape
    r = c_all.shape[0]
    tn = MODS_COL_TILE
    return pl.pallas_call(
        _mods_kernel,
        grid=(depth, n // tn),
        in_specs=[
            pl.BlockSpec((r, D_MODEL), lambda l, j: (0, 0)),
            pl.BlockSpec((None, D_MODEL, tn), lambda l, j: (l, 0, j)),
            pl.BlockSpec((None, 1, tn), lambda l, j: (l, 0, j)),
        ],
        out_specs=pl.BlockSpec((None, r, tn), lambda l, j: (l, 0, j)),
        out_shape=jax.ShapeDtypeStruct((depth, r, n), F32),
        compiler_params=pltpu.CompilerParams(
            dimension_semantics=("arbitrary", "arbitrary"), vmem_limit_bytes=VMEM_LIMIT_BYTES),
        name="adaln_mods",
    )(c_all, ada_w, ada_b.reshape(depth, 1, n))


def _rope_half(t, cos, sin):
    return t * cos + pltpu.roll(t, 64, axis=1) * sin


def _ret_proj_kernel(*refs, rope):
    if rope:
        x_ref, mod_ref, nw_ref, w_ref, cos_ref, sin_ref, q_ref, k_ref, v_ref, g_ref = refs
    else:
        x_ref, mod_ref, nw_ref, w_ref, q_ref, k_ref, v_ref, g_ref = refs
    q_scale = RET_QK_DIM ** -0.5
    half = RET_QK_DIM // 2
    v_lo = 2 * RET_QK_TOTAL

    def norm1(rows):
        return _norm_mod(x_ref[rows, :], nw_ref[...], mod_ref[1:2, :], mod_ref[0:1, :]).astype(BF16)

    def qk_proj(rows, hb, out_ref, base, scale):
        for hd in range(RET_HEADS):
            lo = hd * RET_QK_DIM
            t = _dot(hb, w_ref[:, base + lo:base + lo + RET_QK_DIM])
            if rope:
                t = jnp.concatenate(
                    [_rope_half(t[:, :half], cos_ref[rows, :half], sin_ref[rows, :half]),
                     _rope_half(t[:, half:], cos_ref[rows, half:], sin_ref[rows, half:])], axis=-1)
            if scale is not None:
                t = t * scale
            out_ref[rows, lo:lo + RET_QK_DIM] = t.astype(BF16)

    def vg_proj(rows, hb):
        v_ref[rows, :] = _dot(hb, w_ref[:, v_lo:v_lo + RET_V_TOTAL]).astype(BF16)
        g_ref[rows, :] = _dot(hb, w_ref[:, v_lo + RET_V_TOTAL:]).astype(BF16)

    n_sub = x_ref.shape[0] // SUB_TILE
    rows = [pl.ds(s * SUB_TILE, SUB_TILE) for s in range(n_sub)]
    hb = norm1(rows[0])
    for s in range(n_sub):
        qk_proj(rows[s], hb, q_ref, 0, q_scale)
        hb_next = norm1(rows[s + 1]) if s + 1 < n_sub else None
        qk_proj(rows[s], hb, k_ref, RET_QK_TOTAL, None)
        vg_proj(rows[s], hb)
        hb = hb_next


def _ret_proj(x, mod, nw, w_in, rope_tabs, seq_len):
    t_tokens = x.shape[0]
    tm = TOKEN_TILE
    tiles_per_seq = seq_len // tm
    per_batch = mod.shape[0] > 1
    mod_map = (lambda i: (i // tiles_per_seq, 0, 0)) if per_batch else (lambda i: (0, 0, 0))
    tok = lambda i: (i, 0)
    in_specs = [
        pl.BlockSpec((tm, D_MODEL), tok),
        pl.BlockSpec((None, N_MOD, D_MODEL), mod_map),
        _resident((1, D_MODEL)),
        _resident((D_MODEL, RET_IN_TOTAL)),
    ]
    args = [x, mod, nw.reshape(1, D_MODEL), w_in]
    rope = rope_tabs is not None
    if rope:
        pos = lambda i: (i % tiles_per_seq, 0)
        in_specs += [pl.BlockSpec((tm, RET_QK_DIM), pos), pl.BlockSpec((tm, RET_QK_DIM), pos)]
        args += list(rope_tabs)
    return pl.pallas_call(
        functools.partial(_ret_proj_kernel, rope=rope),
        grid=(t_tokens // tm,),
        in_specs=in_specs,
        out_specs=[
            pl.BlockSpec((tm, RET_QK_TOTAL), tok),
            pl.BlockSpec((tm, RET_QK_TOTAL), tok),
            pl.BlockSpec((tm, RET_V_TOTAL), tok),
            pl.BlockSpec((tm, RET_V_TOTAL), tok),
        ],
        out_shape=[
            jax.ShapeDtypeStruct((t_tokens, RET_QK_TOTAL), BF16),
            jax.ShapeDtypeStruct((t_tokens, RET_QK_TOTAL), BF16),
            jax.ShapeDtypeStruct((t_tokens, RET_V_TOTAL), BF16),
            jax.ShapeDtypeStruct((t_tokens, RET_V_TOTAL), BF16),
        ],
        compiler_params=pltpu.CompilerParams(
            dimension_semantics=("arbitrary",), vmem_limit_bytes=VMEM_LIMIT_BYTES),
        name="ret_proj",
    )(*args)


def _retention_kernel(lg_ref, gnw_ref, wout_ref, qc_ref, kc_ref, vc_ref, gc_ref, ql_ref, kl_ref, vl_ref,
                      gl_ref, oc_ref, ol_ref, y_scr, *, n_ctx, n_lat):
    c = RET_CHUNK
    hd = pl.program_id(1)
    lg_f = -jnp.abs(lg_ref[0, hd])
    lg_b = -jnp.abs(lg_ref[1, hd])
    ii = lax.broadcasted_iota(jnp.int32, (c, c), 0)
    jj = lax.broadcasted_iota(jnp.int32, (c, c), 1)
    diff = (ii - jj).astype(F32)
    causal = diff >= 0
    intra_f = jnp.where(causal, jnp.exp(lg_f * jnp.where(causal, diff, 0.0)), 0.0)
    intra_b = jnp.where(causal, 0.0, jnp.exp(lg_b * jnp.where(causal, 0.0, -diff)))
    idx = lax.broadcasted_iota(jnp.int32, (c, 1), 0).astype(F32)
    qd_f = jnp.exp(lg_f * (idx + 1.0))
    kd_f = jnp.exp(lg_f * (c - 1.0 - idx))
    qd_b = jnp.exp(lg_b * (c - idx))
    kd_b = jnp.exp(lg_b * idx)
    cd_f = jnp.exp(jnp.full((1, 1), lg_f * c, F32))
    cd_b = jnp.exp(jnp.full((1, 1), lg_b * c, F32))
    gnw = gnw_ref[...]

    @pl.when(hd == 0)
    def _():
        oc_ref[...] = jnp.zeros_like(oc_ref)
        ol_ref[...] = jnp.zeros_like(ol_ref)

    def chunk_refs(kind, j):
        rows = pl.ds(j * c, c)
        if kind == "c":
            return qc_ref.at[rows], kc_ref.at[rows], vc_ref.at[rows], gc_ref.at[rows], oc_ref.at[rows]
        return ql_ref.at[rows], kl_ref.at[rows], vl_ref.at[rows], gl_ref.at[rows], ol_ref.at[rows]

    def step(q_r, k_r, v_r, state, intra, qd, kd, cd):
        q, k, v = q_r[...], k_r[...], v_r[...]
        scores = lax.dot_general(q, k, (((1,), (1,)), ((), ())), preferred_element_type=F32)
        p = (scores * intra).astype(BF16)
        out = _dot(p, v)
        kdt = (k.astype(F32) * kd).T.astype(BF16)
        upd = _dot(kdt, v)
        if state is None:
            return out, upd
        out = out + _dot((q.astype(F32) * qd).astype(BF16), state.astype(BF16))
        return out, state * cd + upd

    def finalize(y, g_r, o_r):
        mu = jnp.mean(y, axis=-1, keepdims=True)
        yc = y - mu
        var = jnp.mean(yc * yc, axis=-1, keepdims=True)
        yn = yc * lax.rsqrt(var + GN_EPS) * gnw
        z = (jax.nn.silu(g_r[...].astype(F32)) * yn).astype(BF16)
        o_r[...] += _dot(z, wout_ref[...])

    fwd_order = [("c", j) for j in range(n_ctx)] + [("l", j) for j in range(n_lat)]
    bwd_order = [("c", j) for j in reversed(range(n_ctx))] + [("l", j) for j in reversed(range(n_lat))]
    offset = {"c": 0, "l": n_ctx * c}
    pending = set()
    deferred = []
    s_f = s_b = None
    for key_f, key_b in zip(fwd_order, bwd_order):
        q_r, k_r, v_r, g_f, o_f = chunk_refs(*key_f)
        out_f, s_f = step(q_r, k_r, v_r, s_f, intra_f, qd_f, kd_f, cd_f)
        q_r, k_r, v_r, g_b, o_b = chunk_refs(*key_b)
        out_b, s_b = step(q_r, k_r, v_r, s_b, intra_b, qd_b, kd_b, cd_b)
        for y, g_r, o_r in deferred:
            finalize(y, g_r, o_r)
        deferred = []
        if key_f == key_b:
            deferred.append((out_f + out_b, g_f, o_f))
            continue
        for key, out, g_r, o_r in ((key_f, out_f, g_f, o_f), (key_b, out_b, g_b, o_b)):
            rows = pl.ds(offset[key[0]] + key[1] * c, c)
            if key in pending:
                deferred.append((y_scr[rows, :] + out, g_r, o_r))
            else:
                y_scr[rows, :] = out
                pending.add(key)
    for y, g_r, o_r in deferred:
        finalize(y, g_r, o_r)


def _retention(log_decay, gn_w, w_out, qc, kc, vc, gc, ql, kl, vl, gl):
    bsz, ctx_len, _ = qc.shape
    seq_len = ql.shape[1]
    n_ctx, n_lat = ctx_len // RET_CHUNK, seq_len // RET_CHUNK
    bh = lambda b, h: (b, 0, h)
    qk_c = pl.BlockSpec((None, ctx_len, RET_QK_DIM), bh)
    v_c = pl.BlockSpec((None, ctx_len, RET_V_DIM), bh)
    qk_l = pl.BlockSpec((None, seq_len, RET_QK_DIM), bh)
    v_l = pl.BlockSpec((None, seq_len, RET_V_DIM), bh)
    return pl.pallas_call(
        functools.partial(_retention_kernel, n_ctx=n_ctx, n_lat=n_lat),
        grid=(bsz, RET_HEADS),
        in_specs=[
            pl.BlockSpec(memory_space=pltpu.SMEM),
            pl.BlockSpec((1, RET_V_DIM), lambda b, h: (0, h)),
            pl.BlockSpec((RET_V_DIM, D_MODEL), lambda b, h: (h, 0)),
            qk_c, qk_c, v_c, v_c, qk_l, qk_l, v_l, v_l,
        ],
        out_specs=[
            pl.BlockSpec((None, ctx_len, D_MODEL), lambda b, h: (b, 0, 0)),
            pl.BlockSpec((None, seq_len, D_MODEL), lambda b, h: (b, 0, 0)),
        ],
        out_shape=[
            jax.ShapeDtypeStruct((bsz, ctx_len, D_MODEL), F32),
            jax.ShapeDtypeStruct((bsz, seq_len, D_MODEL), F32),
        ],
        scratch_shapes=[pltpu.VMEM((ctx_len + seq_len, RET_V_DIM), F32)],
        compiler_params=pltpu.CompilerParams(
            dimension_semantics=("arbitrary", "arbitrary"), vmem_limit_bytes=VMEM_LIMIT_BYTES),
        name="retention",
    )(log_decay, gn_w.reshape(1, RET_V_TOTAL), w_out, qc, kc, vc, gc, ql, kl, vl, gl)


def _pool_kernel(x_ref, xp_ref, xn_ref, mod_ref, nw_ref, d_ref, h_scr, *, seq_len):
    tm = x_ref.shape[0]
    i = pl.program_id(1)
    gain = nw_ref[...] * (1.0 + mod_ref[1:2, :])
    shift = mod_ref[0:1, :]
    hal = POOL_HALO
    h = _rmsnorm(x_ref[...], gain) + shift
    h_prev = _rmsnorm(xp_ref[...], gain) + shift
    h_next = _rmsnorm(xn_ref[...], gain) + shift
    h_scr[0:hal, :] = jnp.where(i > 0, h_prev, 0.0)
    h_scr[hal:hal + tm, :] = h
    h_scr[hal + tm:, :] = jnp.where(i < pl.num_programs(1) - 1, h_next, 0.0)
    n_ext = tm + 2 * hal

    def shifted(a, k):
        return pltpu.roll(a, k % n_ext, axis=0)

    t = i * tm + lax.broadcasted_iota(jnp.int32, (tm, 1), 0)
    for gi, w in enumerate(POOL_WINDOWS):
        cols = slice(gi * POOL_GROUP, (gi + 1) * POOL_GROUP)
        e = h_scr[:, cols]
        s = e + shifted(e, 1)
        a = 1
        while 2 * a < w:
            s = shifted(s, a) + shifted(s, -a)
            a *= 2
        cnt = jnp.minimum(t + w // 2, seq_len) - jnp.maximum(t - w // 2, 0)
        d_ref[:, cols] = (s[hal:hal + tm, :] / cnt.astype(F32) - h[:, cols]).astype(BF16)


def _pool(x, mod, nw, seq_len):
    t_tokens = x.shape[0]
    bsz = t_tokens // seq_len
    tm = min(POOL_TILE, seq_len)
    n_tiles = seq_len // tm
    halo_per_tile = tm // POOL_HALO
    n_halo = seq_len // POOL_HALO
    per_batch = mod.shape[0] > 1
    x3 = x.reshape(bsz, seq_len, D_MODEL)
    out = pl.pallas_call(
        functools.partial(_pool_kernel, seq_len=seq_len),
        grid=(bsz, n_tiles),
        in_specs=[
            pl.BlockSpec((None, tm, D_MODEL), lambda b, i: (b, i, 0)),
            pl.BlockSpec((None, POOL_HALO, D_MODEL),
                         lambda b, i: (b, jnp.maximum(i * halo_per_tile - 1, 0), 0)),
            pl.BlockSpec((None, POOL_HALO, D_MODEL),
                         lambda b, i: (b, jnp.minimum((i + 1) * halo_per_tile, n_halo - 1), 0)),
            pl.BlockSpec((None, N_MOD, D_MODEL),
                         (lambda b, i: (b, 0, 0)) if per_batch else (lambda b, i: (0, 0, 0))),
            _resident((1, D_MODEL)),
        ],
        out_specs=pl.BlockSpec((None, tm, D_MODEL), lambda b, i: (b, i, 0)),
        out_shape=jax.ShapeDtypeStruct((bsz, seq_len, D_MODEL), BF16),
        scratch_shapes=[pltpu.VMEM((tm + 2 * POOL_HALO, D_MODEL), F32)],
        compiler_params=pltpu.CompilerParams(
            dimension_semantics=("arbitrary", "arbitrary"), vmem_limit_bytes=VMEM_LIMIT_BYTES),
        name="pool_mix",
    )(x3, x3, x3, mod, nw.reshape(1, D_MODEL))
    return out.reshape(t_tokens, D_MODEL)


def _mix_ffn_kernel(*refs, pool, final):
    refs = list(refs)
    z_ref, x_ref, mod_ref, n2w_ref = refs[:4]
    rest = refs[4:]
    wmix_ref, ps_ref = (rest.pop(0), rest.pop(0)) if pool else (None, None)
    win_ref, wout_ref = rest.pop(0), rest.pop(0)
    fnw_ref = rest.pop(0) if final else None
    o_ref = rest.pop(0)

    hidden = win_ref.shape[-1] // 2

    def mix(rows):
        z = z_ref[rows, :]
        if not pool:
            return z
        return jnp.concatenate(
            [_dot(z[:, g * POOL_GROUP:(g + 1) * POOL_GROUP], wmix_ref[g])
             for g in range(len(POOL_WINDOWS))], axis=-1) * ps_ref[...]

    def norm2(rows, y):
        x1 = x_ref[rows, :] + mod_ref[2:3, :] * y
        h2 = _norm_mod(x1, n2w_ref[...], mod_ref[4:5, :], mod_ref[3:4, :]).astype(BF16)
        return x1, h2

    def gate(ab):
        return (jax.nn.silu(ab[:, :hidden]) * ab[:, hidden:]).astype(BF16)

    def finish(rows, x1, f):
        x2 = x1 + mod_ref[5:6, :] * f
        if final:
            x2 = _rmsnorm(x2, fnw_ref[...])
        o_ref[rows, :] = x2

    assert x_ref.shape[0] == 2 * SUB_TILE
    ra, rb = (pl.ds(s * SUB_TILE, SUB_TILE) for s in range(2))
    y_a = mix(ra)
    y_b = mix(rb)
    x1_a, h2_a = norm2(ra, y_a)
    ab_a = _dot(h2_a, win_ref[...])
    x1_b, h2_b = norm2(rb, y_b)
    ab_b = _dot(h2_b, win_ref[...])
    f_a = _dot(gate(ab_a), wout_ref[...])
    f_b = _dot(gate(ab_b), wout_ref[...])
    finish(ra, x1_a, f_a)
    finish(rb, x1_b, f_b)


def _mix_ffn(z, x, mod, n2w, w_mix, pool_scale, w_ffn_in, w_ffn_out, final_w, seq_len):
    t_tokens = x.shape[0]
    tm = TOKEN_TILE
    tiles_per_seq = seq_len // tm
    per_batch = mod.shape[0] > 1
    mod_map = (lambda i: (i // tiles_per_seq, 0, 0)) if per_batch else (lambda i: (0, 0, 0))
    tok = lambda i: (i, 0)
    pool = pool_scale is not None
    final = final_w is not None
    in_specs = [
        pl.BlockSpec((tm, z.shape[1]), tok),
        pl.BlockSpec((tm, D_MODEL), tok),
        pl.BlockSpec((None, N_MOD, D_MODEL), mod_map),
        _resident((1, D_MODEL)),
    ]
    args = [z, x, mod, n2w.reshape(1, D_MODEL)]
    if pool:
        in_specs += [_resident(w_mix.shape), _resident((1, D_MODEL))]
        args += [w_mix, pool_scale.reshape(1, D_MODEL)]
    in_specs += [_resident(w_ffn_in.shape), _resident(w_ffn_out.shape)]
    args += [w_ffn_in, w_ffn_out]
    if final:
        in_specs.append(_resident((1, D_MODEL)))
        args.append(final_w.reshape(1, D_MODEL))
    return pl.pallas_call(
        functools.partial(_mix_ffn_kernel, pool=pool, final=final),
        grid=(t_tokens // tm,),
        in_specs=in_specs,
        out_specs=pl.BlockSpec((tm, D_MODEL), tok),
        out_shape=jax.ShapeDtypeStruct((t_tokens, D_MODEL), F32),
        compiler_params=pltpu.CompilerParams(
            dimension_semantics=("arbitrary",), vmem_limit_bytes=VMEM_LIMIT_BYTES),
        name="mix_ffn_pool" if pool else "mix_ffn_ret",
    )(*args)


def _rope_tables(seq_len):
    rows = seq_len // GRID_W
    row = jnp.repeat(jnp.arange(rows), GRID_W).astype(F32)
    col = jnp.tile(jnp.arange(GRID_W), rows).astype(F32)
    d = RET_QK_DIM // 2
    inv = ROPE_THETA ** (-jnp.arange(0, d, 2, dtype=F32) / d)
    ang_r = row[:, None] * inv[None, :]
    ang_c = col[:, None] * inv[None, :]
    cos = jnp.concatenate([jnp.cos(ang_r), jnp.cos(ang_r), jnp.cos(ang_c), jnp.cos(ang_c)], axis=-1)
    sin = jnp.concatenate([-jnp.sin(ang_r), jnp.sin(ang_r), -jnp.sin(ang_c), jnp.sin(ang_c)], axis=-1)
    return cos, sin


def kernel(x, c, ctx, c_ctx, ada_w, ada_b, norm1_w, norm2_w, ret_w_in, ret_log_decay, ret_gn_w,
           ret_w_out, pool_w, pool_scale, ffn_w_in, ffn_w_out, final_norm_w):
    bsz, seq_len, _ = x.shape
    ctx_len = ctx.shape[1]
    depth = ada_w.shape[0]

    pad = (-(bsz + 1)) % 8
    c_all = jnp.concatenate([c, c_ctx[None, :], jnp.zeros((pad, D_MODEL), F32)], axis=0)
    mods = _mods(c_all, ada_w, ada_b).reshape(depth, bsz + 1 + pad, N_MOD, D_MODEL)

    rope_tabs = _rope_tables(seq_len)
    ret_w_in_b = ret_w_in.astype(BF16)
    ret_w_out_b = ret_w_out.astype(BF16)
    pool_w_b = pool_w.astype(BF16)
    ffn_w_in_b = ffn_w_in.astype(BF16)
    ffn_w_out_b = ffn_w_out.astype(BF16)

    x_lat = x.reshape(bsz * seq_len, D_MODEL)
    x_ctx = ctx.reshape(bsz * ctx_len, D_MODEL)
    for i in range(depth):
        last = i == depth - 1
        use_ret = (i % N_MIXERS) == 0
        j = i // N_MIXERS
        if last and use_ret:
            raise NotImplementedError("a final retention layer (context-state-only path) is not built")
        mod_lat = mods[i, :bsz]
        mod_ctx = mods[i, bsz:bsz + 1]
        final_w = final_norm_w if last else None
        if use_ret:
            ql, kl, vl, gl = _ret_proj(x_lat, mod_lat, norm1_w[i], ret_w_in_b[j], rope_tabs, seq_len)
            qc, kc, vc, gc = _ret_proj(x_ctx, mod_ctx, norm1_w[i], ret_w_in_b[j], None, ctx_len)
            r3 = lambda a, n: a.reshape(bsz, n, a.shape[-1])
            z_ctx, z_lat = _retention(
                ret_log_decay[j], ret_gn_w[j], ret_w_out_b[j],
                r3(qc, ctx_len), r3(kc, ctx_len), r3(vc, ctx_len), r3(gc, ctx_len),
                r3(ql, seq_len), r3(kl, seq_len), r3(vl, seq_len), r3(gl, seq_len))
            z_lat = z_lat.reshape(bsz * seq_len, D_MODEL)
            z_ctx = z_ctx.reshape(bsz * ctx_len, D_MODEL)
            w_mix, p_scale = None, None
        else:
            z_lat = _pool(x_lat, mod_lat, norm1_w[i], seq_len)
            z_ctx = None if last else _pool(x_ctx, mod_ctx, norm1_w[i], ctx_len)
            w_mix, p_scale = pool_w_b[j], pool_scale[j]
        x_lat = _mix_ffn(z_lat, x_lat, mod_lat, norm2_w[i], w_mix, p_scale,
                         ffn_w_in_b[i], ffn_w_out_b[i], final_w, seq_len)
        if not last:
            x_ctx = _mix_ffn(z_ctx, x_ctx, mod_ctx, norm2_w[i], w_mix, p_scale,
                             ffn_w_in_b[i], ffn_w_out_b[i], None, ctx_len)
    return x_lat.reshape(bsz, seq_len, D_MODEL)
```

```python
import functools

import jax
import jax.numpy as jnp
from jax import lax
from jax.experimental import pallas as pl
from jax.experimental.pallas import tpu as pltpu

D_MODEL = 1024
GRID_W = 64
N_MIXERS = 2
RET_HEADS = 4
RET_QK_DIM = D_MODEL // RET_HEADS
RET_V_DIM = 2 * D_MODEL // RET_HEADS
RET_QK_TOTAL = RET_HEADS * RET_QK_DIM
RET_V_TOTAL = RET_HEADS * RET_V_DIM
RET_IN_TOTAL = 2 * RET_QK_TOTAL + 2 * RET_V_TOTAL
POOL_WINDOWS = (2, 4, 8, 16)
POOL_GROUP = D_MODEL // len(POOL_WINDOWS)
POOL_HALO = 8
ROPE_THETA = 10000.0
EPS = 1e-6
GN_EPS = 1e-5
N_MOD = 6

F32 = jnp.float32
BF16 = jnp.bfloat16

VMEM_LIMIT_BYTES = 56 * 1024 * 1024
TOKEN_TILE = 512
SUB_TILE = 256
POOL_TILE = 256
POOL_NORM_ROWS = 8
POOL_SUM_ROWS = 64
LANES = 128
RET_CHUNK = 256
MODS_COL_TILE = 1536


def _dot(a, b):
    return jnp.dot(a, b, preferred_element_type=F32)


def _rmsnorm(x, w):
    return x * lax.rsqrt(jnp.mean(x * x, axis=-1, keepdims=True) + EPS) * w


def _norm_mod(x, w, scale, shift):
    return _rmsnorm(x, w * (1.0 + scale)) + shift


def _resident(shape):
    nd = len(shape)
    return pl.BlockSpec(shape, lambda *_: (0,) * nd, pipeline_mode=pl.Buffered(1))


def _layer_resident(stacked, layer):
    nd = stacked.ndim - 1
    return pl.BlockSpec((None,) + stacked.shape[1:], lambda *_: (layer,) + (0,) * nd,
                        pipeline_mode=pl.Buffered(1))


def _mods_kernel(c_ref, w_ref, b_ref, o_ref):
    s = jax.nn.silu(c_ref[...]).astype(BF16)
    o_ref[...] = _dot(s, w_ref[...].astype(BF16)) + b_ref[...]


def _mods(c_all, ada_w, ada_b):
    depth, _, n = ada_w.shape
    r = c_all.shape[0]
    tn = MODS_COL_TILE
    return pl.pallas_call(
        _mods_kernel,
        grid=(depth, n // tn),
        in_specs=[
            pl.BlockSpec((r, D_MODEL), lambda l, j: (0, 0)),
            pl.BlockSpec((None, D_MODEL, tn), lambda l, j: (l, 0, j)),
            pl.BlockSpec((None, 1, tn), lambda l, j: (l, 0, j)),
        ],
        out_specs=pl.BlockSpec((None, r, tn), lambda l, j: (l, 0, j)),
        out_shape=jax.ShapeDtypeStruct((depth, r, n), F32),
        compiler_params=pltpu.CompilerParams(
            dimension_semantics=("arbitrary", "arbitrary"), vmem_limit_bytes=VMEM_LIMIT_BYTES),
        name="adaln_mods",
    )(c_all, ada_w, ada_b.reshape(depth, 1, n))


def _rope_half(t, cos, sin):
    return t * cos + pltpu.roll(t, 64, axis=1) * sin


def _ret_proj_kernel(*refs, rope):
    if rope:
        x_ref, mod_ref, nw_ref, w_ref, cos_ref, sin_ref, q_ref, k_ref, v_ref, g_ref = refs
    else:
        x_ref, mod_ref, nw_ref, w_ref, q_ref, k_ref, v_ref, g_ref = refs
    q_scale = RET_QK_DIM ** -0.5
    half = RET_QK_DIM // 2
    v_lo = 2 * RET_QK_TOTAL

    def norm1(rows):
        return _norm_mod(x_ref[rows, :], nw_ref[...], mod_ref[1:2, :], mod_ref[0:1, :]).astype(BF16)

    def qk_proj(rows, hb, out_ref, base, scale):
        for hd in range(RET_HEADS):
            lo = hd * RET_QK_DIM
            t = _dot(hb, w_ref[:, base + lo:base + lo + RET_QK_DIM])
            if rope:
                t = jnp.concatenate(
                    [_rope_half(t[:, :half], cos_ref[rows, :half], sin_ref[rows, :half]),
                     _rope_half(t[:, half:], cos_ref[rows, half:], sin_ref[rows, half:])], axis=-1)
            if scale is not None:
                t = t * scale
            out_ref[rows, lo:lo + RET_QK_DIM] = t.astype(BF16)

    def vg_proj(rows, hb):
        v_ref[rows, :] = _dot(hb, w_ref[:, v_lo:v_lo + RET_V_TOTAL]).astype(BF16)
        g_ref[rows, :] = _dot(hb, w_ref[:, v_lo + RET_V_TOTAL:]).astype(BF16)

    n_sub = x_ref.shape[0] // SUB_TILE
    rows = [pl.ds(s * SUB_TILE, SUB_TILE) for s in range(n_sub)]
    hb = norm1(rows[0])
    for s in range(n_sub):
        qk_proj(rows[s], hb, q_ref, 0, q_scale)
        hb_next = norm1(rows[s + 1]) if s + 1 < n_sub else None
        qk_proj(rows[s], hb, k_ref, RET_QK_TOTAL, None)
        vg_proj(rows[s], hb)
        hb = hb_next


def _ret_proj(x, mod, nw, w_in, layer, rope_tabs, seq_len):
    t_tokens = x.shape[0]
    tm = TOKEN_TILE
    tiles_per_seq = seq_len // tm
    per_batch = mod.shape[0] > 1
    mod_map = (lambda i: (i // tiles_per_seq, 0, 0)) if per_batch else (lambda i: (0, 0, 0))
    tok = lambda i: (i, 0)
    in_specs = [
        pl.BlockSpec((tm, D_MODEL), tok),
        pl.BlockSpec((None, N_MOD, D_MODEL), mod_map),
        _resident((1, D_MODEL)),
        _layer_resident(w_in, layer),
    ]
    args = [x, mod, nw.reshape(1, D_MODEL), w_in]
    rope = rope_tabs is not None
    if rope:
        pos = lambda i: (i % tiles_per_seq, 0)
        in_specs += [pl.BlockSpec((tm, RET_QK_DIM), pos), pl.BlockSpec((tm, RET_QK_DIM), pos)]
        args += list(rope_tabs)
    return pl.pallas_call(
        functools.partial(_ret_proj_kernel, rope=rope),
        grid=(t_tokens // tm,),
        in_specs=in_specs,
        out_specs=[
            pl.BlockSpec((tm, RET_QK_TOTAL), tok),
            pl.BlockSpec((tm, RET_QK_TOTAL), tok),
            pl.BlockSpec((tm, RET_V_TOTAL), tok),
            pl.BlockSpec((tm, RET_V_TOTAL), tok),
        ],
        out_shape=[
            jax.ShapeDtypeStruct((t_tokens, RET_QK_TOTAL), BF16),
            jax.ShapeDtypeStruct((t_tokens, RET_QK_TOTAL), BF16),
            jax.ShapeDtypeStruct((t_tokens, RET_V_TOTAL), BF16),
            jax.ShapeDtypeStruct((t_tokens, RET_V_TOTAL), BF16),
        ],
        compiler_params=pltpu.CompilerParams(
            dimension_semantics=("arbitrary",), vmem_limit_bytes=VMEM_LIMIT_BYTES),
        name="ret_proj",
    )(*args)


def _retention_kernel(lg_ref, gnw_ref, wout_ref, qc_ref, kc_ref, vc_ref, gc_ref, ql_ref, kl_ref, vl_ref,
                      gl_ref, oc_ref, ol_ref, y_scr, *, n_ctx, n_lat):
    c = RET_CHUNK
    hd = pl.program_id(1)
    lg_f = -jnp.abs(lg_ref[0, hd])
    lg_b = -jnp.abs(lg_ref[1, hd])
    ii = lax.broadcasted_iota(jnp.int32, (c, c), 0)
    jj = lax.broadcasted_iota(jnp.int32, (c, c), 1)
    diff = (ii - jj).astype(F32)
    causal = diff >= 0
    intra_f = jnp.where(causal, jnp.exp(lg_f * jnp.where(causal, diff, 0.0)), 0.0)
    intra_b = jnp.where(causal, 0.0, jnp.exp(lg_b * jnp.where(causal, 0.0, -diff)))
    idx = lax.broadcasted_iota(jnp.int32, (c, 1), 0).astype(F32)
    qd_f = jnp.exp(lg_f * (idx + 1.0))
    kd_f = jnp.exp(lg_f * (c - 1.0 - idx))
    qd_b = jnp.exp(lg_b * (c - idx))
    kd_b = jnp.exp(lg_b * idx)
    cd_f = jnp.exp(jnp.full((1, 1), lg_f * c, F32))
    cd_b = jnp.exp(jnp.full((1, 1), lg_b * c, F32))
    gnw = gnw_ref[...]

    @pl.when(hd == 0)
    def _():
        oc_ref[...] = jnp.zeros_like(oc_ref)
        ol_ref[...] = jnp.zeros_like(ol_ref)

    def chunk_refs(kind, j):
        rows = pl.ds(j * c, c)
        if kind == "c":
            return qc_ref.at[rows], kc_ref.at[rows], vc_ref.at[rows], gc_ref.at[rows], oc_ref.at[rows]
        return ql_ref.at[rows], kl_ref.at[rows], vl_ref.at[rows], gl_ref.at[rows], ol_ref.at[rows]

    def step(q_r, k_r, v_r, state, intra, qd, kd, cd):
        q, k, v = q_r[...], k_r[...], v_r[...]
        scores = lax.dot_general(q, k, (((1,), (1,)), ((), ())), preferred_element_type=F32)
        p = (scores * intra).astype(BF16)
        out = _dot(p, v)
        kdt = (k.astype(F32) * kd).T.astype(BF16)
        upd = _dot(kdt, v)
        if state is None:
            return out, upd
        out = out + _dot((q.astype(F32) * qd).astype(BF16), state.astype(BF16))
        return out, state * cd + upd

    def finalize(y, g_r, o_r):
        mu = jnp.mean(y, axis=-1, keepdims=True)
        yc = y - mu
        var = jnp.mean(yc * yc, axis=-1, keepdims=True)
        yn = yc * lax.rsqrt(var + GN_EPS) * gnw
        z = (jax.nn.silu(g_r[...].astype(F32)) * yn).astype(BF16)
        o_r[...] += _dot(z, wout_ref[...])

    fwd_order = [("c", j) for j in range(n_ctx)] + [("l", j) for j in range(n_lat)]
    bwd_order = [("c", j) for j in reversed(range(n_ctx))] + [("l", j) for j in reversed(range(n_lat))]
    offset = {"c": 0, "l": n_ctx * c}
    pending = set()
    deferred = []
    s_f = s_b = None
    for key_f, key_b in zip(fwd_order, bwd_order):
        q_r, k_r, v_r, g_f, o_f = chunk_refs(*key_f)
        out_f, s_f = step(q_r, k_r, v_r, s_f, intra_f, qd_f, kd_f, cd_f)
        q_r, k_r, v_r, g_b, o_b = chunk_refs(*key_b)
        out_b, s_b = step(q_r, k_r, v_r, s_b, intra_b, qd_b, kd_b, cd_b)
        for y, g_r, o_r in deferred:
            finalize(y, g_r, o_r)
        deferred = []
        if key_f == key_b:
            deferred.append((out_f + out_b, g_f, o_f))
            continue
        for key, out, g_r, o_r in ((key_f, out_f, g_f, o_f), (key_b, out_b, g_b, o_b)):
            rows = pl.ds(offset[key[0]] + key[1] * c, c)
            if key in pending:
                deferred.append((y_scr[rows, :] + out, g_r, o_r))
            else:
                y_scr[rows, :] = out
                pending.add(key)
    for y, g_r, o_r in deferred:
        finalize(y, g_r, o_r)


def _retention(log_decay, gn_w, w_out, layer, qc, kc, vc, gc, ql, kl, vl, gl):
    bsz, ctx_len, _ = qc.shape
    seq_len = ql.shape[1]
    n_ctx, n_lat = ctx_len // RET_CHUNK, seq_len // RET_CHUNK
    bh = lambda b, h: (b, 0, h)
    qk_c = pl.BlockSpec((None, ctx_len, RET_QK_DIM), bh)
    v_c = pl.BlockSpec((None, ctx_len, RET_V_DIM), bh)
    qk_l = pl.BlockSpec((None, seq_len, RET_QK_DIM), bh)
    v_l = pl.BlockSpec((None, seq_len, RET_V_DIM), bh)
    return pl.pallas_call(
        functools.partial(_retention_kernel, n_ctx=n_ctx, n_lat=n_lat),
        grid=(bsz, RET_HEADS),
        in_specs=[
            pl.BlockSpec(memory_space=pltpu.SMEM),
            pl.BlockSpec((1, RET_V_DIM), lambda b, h: (0, h)),
            pl.BlockSpec((None, RET_V_DIM, D_MODEL), lambda b, h: (layer, h, 0)),
            qk_c, qk_c, v_c, v_c, qk_l, qk_l, v_l, v_l,
        ],
        out_specs=[
            pl.BlockSpec((None, ctx_len, D_MODEL), lambda b, h: (b, 0, 0)),
            pl.BlockSpec((None, seq_len, D_MODEL), lambda b, h: (b, 0, 0)),
        ],
        out_shape=[
            jax.ShapeDtypeStruct((bsz, ctx_len, D_MODEL), F32),
            jax.ShapeDtypeStruct((bsz, seq_len, D_MODEL), F32),
        ],
        scratch_shapes=[pltpu.VMEM((ctx_len + seq_len, RET_V_DIM), F32)],
        compiler_params=pltpu.CompilerParams(
            dimension_semantics=("arbitrary", "arbitrary"), vmem_limit_bytes=VMEM_LIMIT_BYTES),
        name="retention",
    )(log_decay, gn_w.reshape(1, RET_V_TOTAL), w_out, qc, kc, vc, gc, ql, kl, vl, gl)


def _pool_tile(x_ref, x_prev, x_next, has_prev, has_next, t0, gain, shift, seq_len, h_scr, d_ref):
    tm = x_ref.shape[0]
    hal = POOL_HALO
    h_scr[0:hal, :] = jnp.where(has_prev, _rmsnorm(x_prev, gain) + shift, 0.0)
    for r in range(0, tm, POOL_NORM_ROWS):
        h_scr[hal + r:hal + r + POOL_NORM_ROWS, :] = _rmsnorm(x_ref[r:r + POOL_NORM_ROWS, :], gain) + shift
    h_scr[hal + tm:, :] = jnp.where(has_next, _rmsnorm(x_next, gain) + shift, 0.0)

    pb = min(POOL_SUM_ROWS, tm)
    n_ext = pb + 2 * hal

    def shifted(a, k):
        return pltpu.roll(a, k % n_ext, axis=0)

    for r in range(0, tm, pb):
        t = t0 + r + lax.broadcasted_iota(jnp.int32, (pb, 1), 0)
        for gi, w in enumerate(POOL_WINDOWS):
            cnt = jnp.minimum(t + w // 2, seq_len) - jnp.maximum(t - w // 2, 0)
            inv_cnt = 1.0 / cnt.astype(F32)
            for lo in range(gi * POOL_GROUP, (gi + 1) * POOL_GROUP, LANES):
                e = h_scr[r:r + n_ext, lo:lo + LANES]
                s = e + shifted(e, 1)
                a = 1
                while 2 * a < w:
                    s = shifted(s, a) + shifted(s, -a)
                    a *= 2
                d_ref[r:r + pb, lo:lo + LANES] = (
                    s[hal:hal + pb, :] * inv_cnt - e[hal:hal + pb, :]).astype(BF16)


def _pool_kernel(x_ref, xp_ref, xn_ref, mod_ref, nw_ref, d_ref, h_scr, *, seq_len):
    i = pl.program_id(1)
    gain = nw_ref[...] * (1.0 + mod_ref[1:2, :])
    _pool_tile(x_ref, xp_ref[...], xn_ref[...], i > 0, i < pl.num_programs(1) - 1,
               i * x_ref.shape[0], gain, mod_ref[0:1, :], seq_len, h_scr, d_ref)


def _pool(x, mod, nw, seq_len):
    t_tokens = x.shape[0]
    bsz = t_tokens // seq_len
    tm = min(POOL_TILE, seq_len)
    n_tiles = seq_len // tm
    halo_per_tile = tm // POOL_HALO
    n_halo = seq_len // POOL_HALO
    per_batch = mod.shape[0] > 1
    x3 = x.reshape(bsz, seq_len, D_MODEL)
    out = pl.pallas_call(
        functools.partial(_pool_kernel, seq_len=seq_len),
        grid=(bsz, n_tiles),
        in_specs=[
            pl.BlockSpec((None, tm, D_MODEL), lambda b, i: (b, i, 0)),
            pl.BlockSpec((None, POOL_HALO, D_MODEL),
                         lambda b, i: (b, jnp.maximum(i * halo_per_tile - 1, 0), 0)),
            pl.BlockSpec((None, POOL_HALO, D_MODEL),
                         lambda b, i: (b, jnp.minimum((i + 1) * halo_per_tile, n_halo - 1), 0)),
            pl.BlockSpec((None, N_MOD, D_MODEL),
                         (lambda b, i: (b, 0, 0)) if per_batch else (lambda b, i: (0, 0, 0))),
            _resident((1, D_MODEL)),
        ],
        out_specs=pl.BlockSpec((None, tm, D_MODEL), lambda b, i: (b, i, 0)),
        out_shape=jax.ShapeDtypeStruct((bsz, seq_len, D_MODEL), BF16),
        scratch_shapes=[pltpu.VMEM((tm + 2 * POOL_HALO, D_MODEL), F32)],
        compiler_params=pltpu.CompilerParams(
            dimension_semantics=("arbitrary", "arbitrary"), vmem_limit_bytes=VMEM_LIMIT_BYTES),
        name="pool_mix",
    )(x3, x3, x3, mod, nw.reshape(1, D_MODEL))
    return out.reshape(t_tokens, D_MODEL)


def _pool_mix(d, wmix_ref, ps_ref):
    return jnp.concatenate(
        [_dot(d[:, g * POOL_GROUP:(g + 1) * POOL_GROUP], wmix_ref[g]) for g in range(len(POOL_WINDOWS))],
        axis=-1) * ps_ref[...]


def _ffn_pipeline(mix, x_ref, mod_ref, n2w_ref, win_ref, wout_ref, fnw_ref, o_ref, between=None):
    hidden = win_ref.shape[-1] // 2

    def norm2(rows, y):
        x1 = x_ref[rows, :] + mod_ref[2:3, :] * y
        h2 = _norm_mod(x1, n2w_ref[...], mod_ref[4:5, :], mod_ref[3:4, :]).astype(BF16)
        return x1, h2

    def gate(ab):
        return (jax.nn.silu(ab[:, :hidden]) * ab[:, hidden:]).astype(BF16)

    def finish(rows, x1, f):
        x2 = x1 + mod_ref[5:6, :] * f
        if fnw_ref is not None:
            x2 = _rmsnorm(x2, fnw_ref[...])
        o_ref[rows, :] = x2

    assert x_ref.shape[0] == 2 * SUB_TILE
    ra, rb = (pl.ds(s * SUB_TILE, SUB_TILE) for s in range(2))
    y_a = mix(ra)
    y_b = mix(rb)
    x1_a, h2_a = norm2(ra, y_a)
    ab_a = _dot(h2_a, win_ref[...])
    if between is not None:
        between()
    x1_b, h2_b = norm2(rb, y_b)
    ab_b = _dot(h2_b, win_ref[...])
    f_a = _dot(gate(ab_a), wout_ref[...])
    f_b = _dot(gate(ab_b), wout_ref[...])
    finish(ra, x1_a, f_a)
    finish(rb, x1_b, f_b)


def _mix_ffn_kernel(*refs, pool, final):
    refs = list(refs)
    z_ref, x_ref, mod_ref, n2w_ref = refs[:4]
    rest = refs[4:]
    wmix_ref, ps_ref = (rest.pop(0), rest.pop(0)) if pool else (None, None)
    win_ref, wout_ref = rest.pop(0), rest.pop(0)
    fnw_ref = rest.pop(0) if final else None
    o_ref = rest.pop(0)

    def mix(rows):
        if pool:
            return _pool_mix(z_ref[rows, :], wmix_ref, ps_ref)
        return z_ref[rows, :]

    _ffn_pipeline(mix, x_ref, mod_ref, n2w_ref, win_ref, wout_ref, fnw_ref, o_ref)


def _mix_ffn(z, x, mod, n2w, pool_params, ffn_w_in, ffn_w_out, layer, final_w, seq_len):
    t_tokens = x.shape[0]
    tm = TOKEN_TILE
    tiles_per_seq = seq_len // tm
    per_batch = mod.shape[0] > 1
    mod_map = (lambda i: (i // tiles_per_seq, 0, 0)) if per_batch else (lambda i: (0, 0, 0))
    tok = lambda i: (i, 0)
    pool = pool_params is not None
    final = final_w is not None
    in_specs = [
        pl.BlockSpec((tm, z.shape[1]), tok),
        pl.BlockSpec((tm, D_MODEL), tok),
        pl.BlockSpec((None, N_MOD, D_MODEL), mod_map),
        _resident((1, D_MODEL)),
    ]
    args = [z, x, mod, n2w.reshape(1, D_MODEL)]
    if pool:
        pool_w, pool_idx, pool_scale = pool_params
        in_specs += [_layer_resident(pool_w, pool_idx), _resident((1, D_MODEL))]
        args += [pool_w, pool_scale.reshape(1, D_MODEL)]
    in_specs += [_layer_resident(ffn_w_in, layer), _layer_resident(ffn_w_out, layer)]
    args += [ffn_w_in, ffn_w_out]
    if final:
        in_specs.append(_resident((1, D_MODEL)))
        args.append(final_w.reshape(1, D_MODEL))
    return pl.pallas_call(
        functools.partial(_mix_ffn_kernel, pool=pool, final=final),
        grid=(t_tokens // tm,),
        in_specs=in_specs,
        out_specs=pl.BlockSpec((tm, D_MODEL), tok),
        out_shape=jax.ShapeDtypeStruct((t_tokens, D_MODEL), F32),
        compiler_params=pltpu.CompilerParams(
            dimension_semantics=("arbitrary",), vmem_limit_bytes=VMEM_LIMIT_BYTES),
        name="mix_ffn_pool" if pool else "mix_ffn_ret",
    )(*args)


def _pool_ffn_kernel(*refs, final, seq_len):
    refs = list(refs)
    (xc_ref, xn_ref, xh_ref, mod_ref, modn_ref, n1w_ref, n2w_ref, wmix_ref, ps_ref,
     win_ref, wout_ref) = refs[:11]
    rest = refs[11:]
    fnw_ref = rest.pop(0) if final else None
    o_ref, d_scr, h_scr = rest
    tm = xc_ref.shape[0]
    hal = POOL_HALO
    tiles_per_seq = seq_len // tm
    i = pl.program_id(0)

    @pl.when(i == 0)
    def _():
        gain = n1w_ref[...] * (1.0 + mod_ref[1:2, :])
        _pool_tile(xc_ref, xc_ref[0:hal, :], xn_ref[0:hal, :], False, tiles_per_seq > 1,
                   0, gain, mod_ref[0:1, :], seq_len, h_scr, d_scr)

    def pool_next():
        j = (i + 1) % tiles_per_seq
        gain = n1w_ref[...] * (1.0 + modn_ref[1:2, :])
        _pool_tile(xn_ref, xc_ref[tm - hal:tm, :], xh_ref[...], j > 0, j < tiles_per_seq - 1,
                   j * tm, gain, modn_ref[0:1, :], seq_len, h_scr, d_scr)

    _ffn_pipeline(lambda rows: _pool_mix(d_scr[rows, :], wmix_ref, ps_ref),
                  xc_ref, mod_ref, n2w_ref, win_ref, wout_ref, fnw_ref, o_ref, between=pool_next)


def _pool_ffn(x, mod, n1w, n2w, pool_params, ffn_w_in, ffn_w_out, layer, final_w, seq_len):
    t_tokens = x.shape[0]
    tm = TOKEN_TILE
    n_steps = t_tokens // tm
    tiles_per_seq = seq_len // tm
    halo_per_tile = tm // POOL_HALO
    pool_w, pool_idx, pool_scale = pool_params
    final = final_w is not None
    nxt = lambda i: jnp.minimum(i + 1, n_steps - 1)
    in_specs = [
        pl.BlockSpec((tm, D_MODEL), lambda i: (i, 0)),
        pl.BlockSpec((tm, D_MODEL), lambda i: (nxt(i), 0)),
        pl.BlockSpec((POOL_HALO, D_MODEL), lambda i: (jnp.minimum(i + 2, n_steps - 1) * halo_per_tile, 0)),
        pl.BlockSpec((None, N_MOD, D_MODEL), lambda i: (i // tiles_per_seq, 0, 0)),
        pl.BlockSpec((None, N_MOD, D_MODEL), lambda i: (nxt(i) // tiles_per_seq, 0, 0)),
        _resident((1, D_MODEL)),
        _resident((1, D_MODEL)),
        _layer_resident(pool_w, pool_idx),
        _resident((1, D_MODEL)),
        _layer_resident(ffn_w_in, layer),
        _layer_resident(ffn_w_out, layer),
    ]
    args = [x, x, x, mod, mod, n1w.reshape(1, D_MODEL), n2w.reshape(1, D_MODEL), pool_w,
            pool_scale.reshape(1, D_MODEL), ffn_w_in, ffn_w_out]
    if final:
        in_specs.append(_resident((1, D_MODEL)))
        args.append(final_w.reshape(1, D_MODEL))
    return pl.pallas_call(
        functools.partial(_pool_ffn_kernel, final=final, seq_len=seq_len),
        grid=(n_steps,),
        in_specs=in_specs,
        out_specs=pl.BlockSpec((tm, D_MODEL), lambda i: (i, 0)),
        out_shape=jax.ShapeDtypeStruct((t_tokens, D_MODEL), F32),
        scratch_shapes=[pltpu.VMEM((tm, D_MODEL), BF16),
                        pltpu.VMEM((tm + 2 * POOL_HALO, D_MODEL), F32)],
        compiler_params=pltpu.CompilerParams(
            dimension_semantics=("arbitrary",), vmem_limit_bytes=VMEM_LIMIT_BYTES),
        name="pool_ffn",
    )(*args)


def _rope_tables(seq_len):
    rows = seq_len // GRID_W
    row = jnp.repeat(jnp.arange(rows), GRID_W).astype(F32)
    col = jnp.tile(jnp.arange(GRID_W), rows).astype(F32)
    d = RET_QK_DIM // 2
    inv = ROPE_THETA ** (-jnp.arange(0, d, 2, dtype=F32) / d)
    ang_r = row[:, None] * inv[None, :]
    ang_c = col[:, None] * inv[None, :]
    cos = jnp.concatenate([jnp.cos(ang_r), jnp.cos(ang_r), jnp.cos(ang_c), jnp.cos(ang_c)], axis=-1)
    sin = jnp.concatenate([-jnp.sin(ang_r), jnp.sin(ang_r), -jnp.sin(ang_c), jnp.sin(ang_c)], axis=-1)
    return cos, sin


def kernel(x, c, ctx, c_ctx, ada_w, ada_b, norm1_w, norm2_w, ret_w_in, ret_log_decay, ret_gn_w,
           ret_w_out, pool_w, pool_scale, ffn_w_in, ffn_w_out, final_norm_w):
    bsz, seq_len, _ = x.shape
    ctx_len = ctx.shape[1]
    depth = ada_w.shape[0]

    pad = (-(bsz + 1)) % 8
    c_all = jnp.concatenate([c, c_ctx[None, :], jnp.zeros((pad, D_MODEL), F32)], axis=0)
    mods = _mods(c_all, ada_w, ada_b).reshape(depth, bsz + 1 + pad, N_MOD, D_MODEL)

    rope_tabs = _rope_tables(seq_len)
    ret_w_in_b = ret_w_in.astype(BF16)
    ret_w_out_b = ret_w_out.astype(BF16)
    pool_w_b = pool_w.astype(BF16)
    ffn_w_in_b = ffn_w_in.astype(BF16)
    ffn_w_out_b = ffn_w_out.astype(BF16)

    x_lat = x.reshape(bsz * seq_len, D_MODEL)
    x_ctx = ctx.reshape(bsz * ctx_len, D_MODEL)
    for i in range(depth):
        last = i == depth - 1
        use_ret = (i % N_MIXERS) == 0
        j = i // N_MIXERS
        if last and use_ret:
            raise NotImplementedError("a final retention layer (context-state-only path) is not built")
        mod_lat = mods[i, :bsz]
        mod_ctx = mods[i, bsz:bsz + 1]
        final_w = final_norm_w if last else None
        if use_ret:
            ql, kl, vl, gl = _ret_proj(x_lat, mod_lat, norm1_w[i], ret_w_in_b, j, rope_tabs, seq_len)
            qc, kc, vc, gc = _ret_proj(x_ctx, mod_ctx, norm1_w[i], ret_w_in_b, j, None, ctx_len)
            r3 = lambda a, n: a.reshape(bsz, n, a.shape[-1])
            o_ctx, o_lat = _retention(
                ret_log_decay[j], ret_gn_w[j], ret_w_out_b, j,
                r3(qc, ctx_len), r3(kc, ctx_len), r3(vc, ctx_len), r3(gc, ctx_len),
                r3(ql, seq_len), r3(kl, seq_len), r3(vl, seq_len), r3(gl, seq_len))
            x_lat = _mix_ffn(o_lat.reshape(bsz * seq_len, D_MODEL), x_lat, mod_lat, norm2_w[i], None,
                             ffn_w_in_b, ffn_w_out_b, i, final_w, seq_len)
            x_ctx = _mix_ffn(o_ctx.reshape(bsz * ctx_len, D_MODEL), x_ctx, mod_ctx, norm2_w[i], None,
                             ffn_w_in_b, ffn_w_out_b, i, None, ctx_len)
        else:
            pool_params = (pool_w_b, j, pool_scale[j])
            if not last:
                d_ctx = _pool(x_ctx, mod_ctx, norm1_w[i], ctx_len)
                x_ctx = _mix_ffn(d_ctx, x_ctx, mod_ctx, norm2_w[i], pool_params,
                                 ffn_w_in_b, ffn_w_out_b, i, None, ctx_len)
            x_lat = _pool_ffn(x_lat, mod_lat, norm1_w[i], norm2_w[i], pool_params,
                              ffn_w_in_b, ffn_w_out_b, i, final_w, seq_len)
    return x_lat.reshape(bsz, seq_len, D_MODEL)
```

```python
import functools

import jax
import jax.numpy as jnp
from jax import lax
from jax.experimental import pallas as pl
from jax.experimental.pallas import tpu as pltpu

D_MODEL = 1024
GRID_W = 64
N_MIXERS = 2
RET_HEADS = 4
RET_QK_DIM = D_MODEL // RET_HEADS
RET_V_DIM = 2 * D_MODEL // RET_HEADS
RET_QK_TOTAL = RET_HEADS * RET_QK_DIM
RET_V_TOTAL = RET_HEADS * RET_V_DIM
RET_IN_TOTAL = 2 * RET_QK_TOTAL + 2 * RET_V_TOTAL
POOL_WINDOWS = (2, 4, 8, 16)
POOL_GROUP = D_MODEL // len(POOL_WINDOWS)
POOL_HALO = 8
ROPE_THETA = 10000.0
EPS = 1e-6
GN_EPS = 1e-5
N_MOD = 6

F32 = jnp.float32
BF16 = jnp.bfloat16

VMEM_LIMIT_BYTES = 56 * 1024 * 1024
TOKEN_TILE = 512
SUB_TILE = 256
POOL_TILE = 256
POOL_NORM_ROWS = 8
POOL_SUM_ROWS = 64
LANES = 128
RET_CHUNK = 256
MODS_COL_TILE = 1536


def _dot(a, b):
    return jnp.dot(a, b, preferred_element_type=F32)


def _rmsnorm(x, w):
    return x * lax.rsqrt(jnp.mean(x * x, axis=-1, keepdims=True) + EPS) * w


def _norm_mod(x, w, scale, shift):
    return _rmsnorm(x, w * (1.0 + scale)) + shift


def _resident(shape):
    nd = len(shape)
    return pl.BlockSpec(shape, lambda *_: (0,) * nd, pipeline_mode=pl.Buffered(1))


def _layer_resident(stacked, layer):
    nd = stacked.ndim - 1
    return pl.BlockSpec((None,) + stacked.shape[1:], lambda *_: (layer,) + (0,) * nd,
                        pipeline_mode=pl.Buffered(1))


def _mods_kernel(c_ref, w_ref, b_ref, o_ref):
    s = jax.nn.silu(c_ref[...]).astype(BF16)
    o_ref[...] = _dot(s, w_ref[...].astype(BF16)) + b_ref[...]


def _mods(c_all, ada_w, ada_b):
    depth, _, n = ada_w.shape
    r = c_all.shape[0]
    tn = MODS_COL_TILE
    return pl.pallas_call(
        _mods_kernel,
        grid=(depth, n // tn),
        in_specs=[
            pl.BlockSpec((r, D_MODEL), lambda l, j: (0, 0)),
            pl.BlockSpec((None, D_MODEL, tn), lambda l, j: (l, 0, j)),
            pl.BlockSpec((None, 1, tn), lambda l, j: (l, 0, j)),
        ],
        out_specs=pl.BlockSpec((None, r, tn), lambda l, j: (l, 0, j)),
        out_shape=jax.ShapeDtypeStruct((depth, r, n), F32),
        compiler_params=pltpu.CompilerParams(
            dimension_semantics=("arbitrary", "arbitrary"), vmem_limit_bytes=VMEM_LIMIT_BYTES),
        name="adaln_mods",
    )(c_all, ada_w, ada_b.reshape(depth, 1, n))


def _rope_half(t, cos, sin):
    return t * cos + pltpu.roll(t, 64, axis=1) * sin


def _chunk_decays(lg_ref, head):
    c = RET_CHUNK
    lg_f = -jnp.abs(lg_ref[0, head])
    lg_b = -jnp.abs(lg_ref[1, head])
    idx = lax.broadcasted_iota(jnp.int32, (c, 1), 0).astype(F32)
    return (jnp.exp(lg_f * (idx + 1.0)), jnp.exp(lg_f * (c - 1.0 - idx)),
            jnp.exp(lg_b * (c - idx)), jnp.exp(lg_b * idx))


def _ret_proj_kernel(*refs, rope):
    refs = list(refs)
    lg_ref, x_ref, mod_ref, nw_ref, w_ref = refs[:5]
    rest = refs[5:]
    cos_ref, sin_ref = (rest.pop(0), rest.pop(0)) if rope else (None, None)
    q_ref, qf_ref, qb_ref, k_ref, kf_ref, kb_ref, v_ref, g_ref = rest
    q_scale = RET_QK_DIM ** -0.5
    half = RET_QK_DIM // 2
    v_lo = 2 * RET_QK_TOTAL
    assert SUB_TILE == RET_CHUNK
    decays = [_chunk_decays(lg_ref, hd) for hd in range(RET_HEADS)]

    def norm1(rows):
        return _norm_mod(x_ref[rows, :], nw_ref[...], mod_ref[1:2, :], mod_ref[0:1, :]).astype(BF16)

    def qk_proj(rows, hb, base, scale, outs, which):
        for hd in range(RET_HEADS):
            lo = hd * RET_QK_DIM
            cols = slice(lo, lo + RET_QK_DIM)
            t = _dot(hb, w_ref[:, base + lo:base + lo + RET_QK_DIM])
            if rope:
                t = jnp.concatenate(
                    [_rope_half(t[:, :half], cos_ref[rows, :half], sin_ref[rows, :half]),
                     _rope_half(t[:, half:], cos_ref[rows, half:], sin_ref[rows, half:])], axis=-1)
            if scale is not None:
                t = t * scale
            plain_ref, fwd_ref, bwd_ref = outs
            plain_ref[rows, cols] = t.astype(BF16)
            fwd_ref[rows, cols] = (t * decays[hd][which]).astype(BF16)
            bwd_ref[rows, cols] = (t * decays[hd][which + 2]).astype(BF16)

    def vg_proj(rows, hb):
        v_ref[rows, :] = _dot(hb, w_ref[:, v_lo:v_lo + RET_V_TOTAL]).astype(BF16)
        g_ref[rows, :] = jax.nn.silu(_dot(hb, w_ref[:, v_lo + RET_V_TOTAL:])).astype(BF16)

    n_sub = x_ref.shape[0] // SUB_TILE
    rows = [pl.ds(s * SUB_TILE, SUB_TILE) for s in range(n_sub)]
    hb = norm1(rows[0])
    for s in range(n_sub):
        qk_proj(rows[s], hb, 0, q_scale, (q_ref, qf_ref, qb_ref), 0)
        hb_next = norm1(rows[s + 1]) if s + 1 < n_sub else None
        qk_proj(rows[s], hb, RET_QK_TOTAL, None, (k_ref, kf_ref, kb_ref), 1)
        vg_proj(rows[s], hb)
        hb = hb_next


def _ret_proj(x, mod, nw, w_in, layer, log_decay, rope_tabs, seq_len):
    t_tokens = x.shape[0]
    tm = TOKEN_TILE
    tiles_per_seq = seq_len // tm
    per_batch = mod.shape[0] > 1
    mod_map = (lambda i: (i // tiles_per_seq, 0, 0)) if per_batch else (lambda i: (0, 0, 0))
    tok = lambda i: (i, 0)
    in_specs = [
        pl.BlockSpec(memory_space=pltpu.SMEM),
        pl.BlockSpec((tm, D_MODEL), tok),
        pl.BlockSpec((None, N_MOD, D_MODEL), mod_map),
        _resident((1, D_MODEL)),
        _layer_resident(w_in, layer),
    ]
    args = [log_decay, x, mod, nw.reshape(1, D_MODEL), w_in]
    rope = rope_tabs is not None
    if rope:
        pos = lambda i: (i % tiles_per_seq, 0)
        in_specs += [pl.BlockSpec((tm, RET_QK_DIM), pos), pl.BlockSpec((tm, RET_QK_DIM), pos)]
        args += list(rope_tabs)
    widths = [RET_QK_TOTAL] * 6 + [RET_V_TOTAL] * 2
    return pl.pallas_call(
        functools.partial(_ret_proj_kernel, rope=rope),
        grid=(t_tokens // tm,),
        in_specs=in_specs,
        out_specs=[pl.BlockSpec((tm, w), tok) for w in widths],
        out_shape=[jax.ShapeDtypeStruct((t_tokens, w), BF16) for w in widths],
        compiler_params=pltpu.CompilerParams(
            dimension_semantics=("arbitrary",), vmem_limit_bytes=VMEM_LIMIT_BYTES),
        name="ret_proj",
    )(*args)


def _retention_kernel(lg_ref, gnw_ref, wout_ref, *refs, n_ctx, n_lat):
    ctx_refs, lat_refs = refs[:8], refs[8:16]
    oc_ref, ol_ref, y_scr = refs[16:]
    c = RET_CHUNK
    hd = pl.program_id(1)
    lg_f = -jnp.abs(lg_ref[0, hd])
    lg_b = -jnp.abs(lg_ref[1, hd])
    ii = lax.broadcasted_iota(jnp.int32, (c, c), 0)
    jj = lax.broadcasted_iota(jnp.int32, (c, c), 1)
    diff = (ii - jj).astype(F32)
    intra = jnp.exp(jnp.where(diff >= 0, lg_f * diff, lg_b * (-diff)))
    cd_f = jnp.exp(jnp.full((1, 1), lg_f * c, F32))
    cd_b = jnp.exp(jnp.full((1, 1), lg_b * c, F32))
    gnw = gnw_ref[...]

    @pl.when(hd == 0)
    def _():
        oc_ref[...] = jnp.zeros_like(oc_ref)
        ol_ref[...] = jnp.zeros_like(ol_ref)

    def chunk_refs(kind, j):
        rows = pl.ds(j * c, c)
        src, o_ref = (ctx_refs, oc_ref) if kind == "c" else (lat_refs, ol_ref)
        return [r.at[rows] for r in src], o_ref.at[rows]

    def inter(qd_r, kd_r, v_r, state, cd):
        upd = _dot(kd_r[...].T, v_r[...])
        if state is None:
            return None, upd
        return _dot(qd_r[...], state.astype(BF16)), state * cd + upd

    def intra_chunk(q_r, k_r, v_r):
        scores = lax.dot_general(q_r[...], k_r[...], (((1,), (1,)), ((), ())), preferred_element_type=F32)
        return _dot((scores * intra).astype(BF16), v_r[...])

    def finalize(y, g_r, o_r):
        mu = jnp.mean(y, axis=-1, keepdims=True)
        yc = y - mu
        var = jnp.mean(yc * yc, axis=-1, keepdims=True)
        yn = yc * lax.rsqrt(var + GN_EPS) * gnw
        z = (g_r[...].astype(F32) * yn).astype(BF16)
        o_r[...] += _dot(z, wout_ref[...])

    fwd_order = [("c", j) for j in range(n_ctx)] + [("l", j) for j in range(n_lat)]
    bwd_order = [("c", j) for j in reversed(range(n_ctx))] + [("l", j) for j in reversed(range(n_lat))]
    offset = {"c": 0, "l": n_ctx * c}
    first_visit = {}
    deferred = []
    s_f = s_b = None
    for key_f, key_b in zip(fwd_order, bwd_order):
        (q_r, qf_r, _, k_r, kf_r, _, v_r, g_f), o_f = chunk_refs(*key_f)
        out_f, s_f = inter(qf_r, kf_r, v_r, s_f, cd_f)
        out_i = intra_chunk(q_r, k_r, v_r)
        out_f = out_i if out_f is None else out_i + out_f
        (_, _, qb_r, _, _, kb_r, v_r, g_b), o_b = chunk_refs(*key_b)
        out_b, s_b = inter(qb_r, kb_r, v_r, s_b, cd_b)
        for y, g_r, o_r in deferred:
            finalize(y, g_r, o_r)
        deferred = []
        if key_f == key_b:
            deferred.append((out_f if out_b is None else out_f + out_b, g_f, o_f))
            continue
        for key, out, g_r, o_r in ((key_f, out_f, g_f, o_f), (key_b, out_b, g_b, o_b)):
            rows = pl.ds(offset[key[0]] + key[1] * c, c)
            if key not in first_visit:
                if out is not None:
                    y_scr[rows, :] = out
                first_visit[key] = out is not None
            elif not first_visit[key]:
                deferred.append((out, g_r, o_r))
            else:
                deferred.append((y_scr[rows, :] if out is None else y_scr[rows, :] + out, g_r, o_r))
    for y, g_r, o_r in deferred:
        finalize(y, g_r, o_r)


def _retention(log_decay, gn_w, w_out, layer, ctx_arrays, lat_arrays):
    bsz, ctx_len, _ = ctx_arrays[0].shape
    seq_len = lat_arrays[0].shape[1]
    n_ctx, n_lat = ctx_len // RET_CHUNK, seq_len // RET_CHUNK
    bh = lambda b, h: (b, 0, h)
    head_specs = lambda length: ([pl.BlockSpec((None, length, RET_QK_DIM), bh)] * 6
                                 + [pl.BlockSpec((None, length, RET_V_DIM), bh)] * 2)
    return pl.pallas_call(
        functools.partial(_retention_kernel, n_ctx=n_ctx, n_lat=n_lat),
        grid=(bsz, RET_HEADS),
        in_specs=[
            pl.BlockSpec(memory_space=pltpu.SMEM),
            pl.BlockSpec((1, RET_V_DIM), lambda b, h: (0, h)),
            pl.BlockSpec((None, RET_V_DIM, D_MODEL), lambda b, h: (layer, h, 0)),
        ] + head_specs(ctx_len) + head_specs(seq_len),
        out_specs=[
            pl.BlockSpec((None, ctx_len, D_MODEL), lambda b, h: (b, 0, 0)),
            pl.BlockSpec((None, seq_len, D_MODEL), lambda b, h: (b, 0, 0)),
        ],
        out_shape=[
            jax.ShapeDtypeStruct((bsz, ctx_len, D_MODEL), F32),
            jax.ShapeDtypeStruct((bsz, seq_len, D_MODEL), F32),
        ],
        scratch_shapes=[pltpu.VMEM((ctx_len + seq_len, RET_V_DIM), F32)],
        compiler_params=pltpu.CompilerParams(
            dimension_semantics=("arbitrary", "arbitrary"), vmem_limit_bytes=VMEM_LIMIT_BYTES),
        name="retention",
    )(log_decay, gn_w.reshape(1, RET_V_TOTAL), w_out, *ctx_arrays, *lat_arrays)


def _pool_tile(x_ref, x_prev, x_next, has_prev, has_next, t0, gain, shift, seq_len, h_scr, d_ref):
    tm = x_ref.shape[0]
    hal = POOL_HALO
    h_scr[0:hal, :] = jnp.where(has_prev, _rmsnorm(x_prev, gain) + shift, 0.0)
    for r in range(0, tm, POOL_NORM_ROWS):
        h_scr[hal + r:hal + r + POOL_NORM_ROWS, :] = _rmsnorm(x_ref[r:r + POOL_NORM_ROWS, :], gain) + shift
    h_scr[hal + tm:, :] = jnp.where(has_next, _rmsnorm(x_next, gain) + shift, 0.0)

    pb = min(POOL_SUM_ROWS, tm)
    n_ext = pb + 2 * hal

    def shifted(a, k):
        return pltpu.roll(a, k % n_ext, axis=0)

    for r in range(0, tm, pb):
        t = t0 + r + lax.broadcasted_iota(jnp.int32, (pb, 1), 0)
        for gi, w in enumerate(POOL_WINDOWS):
            cnt = jnp.minimum(t + w // 2, seq_len) - jnp.maximum(t - w // 2, 0)
            inv_cnt = 1.0 / cnt.astype(F32)
            for lo in range(gi * POOL_GROUP, (gi + 1) * POOL_GROUP, LANES):
                e = h_scr[r:r + n_ext, lo:lo + LANES]
                s = e + shifted(e, 1)
                a = 1
                while 2 * a < w:
                    s = shifted(s, a) + shifted(s, -a)
                    a *= 2
                d_ref[r:r + pb, lo:lo + LANES] = (
                    s[hal:hal + pb, :] * inv_cnt - e[hal:hal + pb, :]).astype(BF16)


def _pool_kernel(x_ref, xp_ref, xn_ref, mod_ref, nw_ref, d_ref, h_scr, *, seq_len):
    i = pl.program_id(1)
    gain = nw_ref[...] * (1.0 + mod_ref[1:2, :])
    _pool_tile(x_ref, xp_ref[...], xn_ref[...], i > 0, i < pl.num_programs(1) - 1,
               i * x_ref.shape[0], gain, mod_ref[0:1, :], seq_len, h_scr, d_ref)


def _pool(x, mod, nw, seq_len):
    t_tokens = x.shape[0]
    bsz = t_tokens // seq_len
    tm = min(POOL_TILE, seq_len)
    n_tiles = seq_len // tm
    halo_per_tile = tm // POOL_HALO
    n_halo = seq_len // POOL_HALO
    per_batch = mod.shape[0] > 1
    x3 = x.reshape(bsz, seq_len, D_MODEL)
    out = pl.pallas_call(
        functools.partial(_pool_kernel, seq_len=seq_len),
        grid=(bsz, n_tiles),
        in_specs=[
            pl.BlockSpec((None, tm, D_MODEL), lambda b, i: (b, i, 0)),
            pl.BlockSpec((None, POOL_HALO, D_MODEL),
                         lambda b, i: (b, jnp.maximum(i * halo_per_tile - 1, 0), 0)),
            pl.BlockSpec((None, POOL_HALO, D_MODEL),
                         lambda b, i: (b, jnp.minimum((i + 1) * halo_per_tile, n_halo - 1), 0)),
            pl.BlockSpec((None, N_MOD, D_MODEL),
                         (lambda b, i: (b, 0, 0)) if per_batch else (lambda b, i: (0, 0, 0))),
            _resident((1, D_MODEL)),
        ],
        out_specs=pl.BlockSpec((None, tm, D_MODEL), lambda b, i: (b, i, 0)),
        out_shape=jax.ShapeDtypeStruct((bsz, seq_len, D_MODEL), BF16),
        scratch_shapes=[pltpu.VMEM((tm + 2 * POOL_HALO, D_MODEL), F32)],
        compiler_params=pltpu.CompilerParams(
            dimension_semantics=("arbitrary", "arbitrary"), vmem_limit_bytes=VMEM_LIMIT_BYTES),
        name="pool_mix",
    )(x3, x3, x3, mod, nw.reshape(1, D_MODEL))
    return out.reshape(t_tokens, D_MODEL)


def _pool_mix(d, wmix_ref, ps_ref):
    return jnp.concatenate(
        [_dot(d[:, g * POOL_GROUP:(g + 1) * POOL_GROUP], wmix_ref[g]) for g in range(len(POOL_WINDOWS))],
        axis=-1) * ps_ref[...]


def _ffn_pipeline(mix, x_ref, mod_ref, n2w_ref, win_ref, wout_ref, fnw_ref, o_ref, between=None):
    hidden = win_ref.shape[-1] // 2

    def norm2(rows, y):
        x1 = x_ref[rows, :] + mod_ref[2:3, :] * y
        h2 = _norm_mod(x1, n2w_ref[...], mod_ref[4:5, :], mod_ref[3:4, :]).astype(BF16)
        return x1, h2

    def gate(ab):
        return (jax.nn.silu(ab[:, :hidden]) * ab[:, hidden:]).astype(BF16)

    def finish(rows, x1, f):
        x2 = x1 + mod_ref[5:6, :] * f
        if fnw_ref is not None:
            x2 = _rmsnorm(x2, fnw_ref[...])
        o_ref[rows, :] = x2

    assert x_ref.shape[0] == 2 * SUB_TILE
    ra, rb = (pl.ds(s * SUB_TILE, SUB_TILE) for s in range(2))
    y_a = mix(ra)
    y_b = mix(rb)
    x1_a, h2_a = norm2(ra, y_a)
    ab_a = _dot(h2_a, win_ref[...])
    if between is not None:
        between()
    x1_b, h2_b = norm2(rb, y_b)
    ab_b = _dot(h2_b, win_ref[...])
    f_a = _dot(gate(ab_a), wout_ref[...])
    f_b = _dot(gate(ab_b), wout_ref[...])
    finish(ra, x1_a, f_a)
    finish(rb, x1_b, f_b)


def _mix_ffn_kernel(*refs, pool, final):
    refs = list(refs)
    z_ref, x_ref, mod_ref, n2w_ref = refs[:4]
    rest = refs[4:]
    wmix_ref, ps_ref = (rest.pop(0), rest.pop(0)) if pool else (None, None)
    win_ref, wout_ref = rest.pop(0), rest.pop(0)
    fnw_ref = rest.pop(0) if final else None
    o_ref = rest.pop(0)

    def mix(rows):
        if pool:
            return _pool_mix(z_ref[rows, :], wmix_ref, ps_ref)
        return z_ref[rows, :]

    _ffn_pipeline(mix, x_ref, mod_ref, n2w_ref, win_ref, wout_ref, fnw_ref, o_ref)


def _mix_ffn(z, x, mod, n2w, pool_params, ffn_w_in, ffn_w_out, layer, final_w, seq_len):
    t_tokens = x.shape[0]
    tm = TOKEN_TILE
    tiles_per_seq = seq_len // tm
    per_batch = mod.shape[0] > 1
    mod_map = (lambda i: (i // tiles_per_seq, 0, 0)) if per_batch else (lambda i: (0, 0, 0))
    tok = lambda i: (i, 0)
    pool = pool_params is not None
    final = final_w is not None
    in_specs = [
        pl.BlockSpec((tm, z.shape[1]), tok),
        pl.BlockSpec((tm, D_MODEL), tok),
        pl.BlockSpec((None, N_MOD, D_MODEL), mod_map),
        _resident((1, D_MODEL)),
    ]
    args = [z, x, mod, n2w.reshape(1, D_MODEL)]
    if pool:
        pool_w, pool_idx, pool_scale = pool_params
        in_specs += [_layer_resident(pool_w, pool_idx), _resident((1, D_MODEL))]
        args += [pool_w, pool_scale.reshape(1, D_MODEL)]
    in_specs += [_layer_resident(ffn_w_in, layer), _layer_resident(ffn_w_out, layer)]
    args += [ffn_w_in, ffn_w_out]
    if final:
        in_specs.append(_resident((1, D_MODEL)))
        args.append(final_w.reshape(1, D_MODEL))
    return pl.pallas_call(
        functools.partial(_mix_ffn_kernel, pool=pool, final=final),
        grid=(t_tokens // tm,),
        in_specs=in_specs,
        out_specs=pl.BlockSpec((tm, D_MODEL), tok),
        out_shape=jax.ShapeDtypeStruct((t_tokens, D_MODEL), F32),
        compiler_params=pltpu.CompilerParams(
            dimension_semantics=("arbitrary",), vmem_limit_bytes=VMEM_LIMIT_BYTES),
        name="mix_ffn_pool" if pool else "mix_ffn_ret",
    )(*args)


def _pool_ffn_kernel(*refs, final, seq_len):
    refs = list(refs)
    (xc_ref, xn_ref, xh_ref, mod_ref, modn_ref, n1w_ref, n2w_ref, wmix_ref, ps_ref,
     win_ref, wout_ref) = refs[:11]
    rest = refs[11:]
    fnw_ref = rest.pop(0) if final else None
    o_ref, d_scr, h_scr = rest
    tm = xc_ref.shape[0]
    hal = POOL_HALO
    tiles_per_seq = seq_len // tm
    i = pl.program_id(0)

    @pl.when(i == 0)
    def _():
        gain = n1w_ref[...] * (1.0 + mod_ref[1:2, :])
        _pool_tile(xc_ref, xc_ref[0:hal, :], xn_ref[0:hal, :], False, tiles_per_seq > 1,
                   0, gain, mod_ref[0:1, :], seq_len, h_scr, d_scr)

    def pool_next():
        j = (i + 1) % tiles_per_seq
        gain = n1w_ref[...] * (1.0 + modn_ref[1:2, :])
        _pool_tile(xn_ref, xc_ref[tm - hal:tm, :], xh_ref[...], j > 0, j < tiles_per_seq - 1,
                   j * tm, gain, modn_ref[0:1, :], seq_len, h_scr, d_scr)

    _ffn_pipeline(lambda rows: _pool_mix(d_scr[rows, :], wmix_ref, ps_ref),
                  xc_ref, mod_ref, n2w_ref, win_ref, wout_ref, fnw_ref, o_ref, between=pool_next)


def _pool_ffn(x, mod, n1w, n2w, pool_params, ffn_w_in, ffn_w_out, layer, final_w, seq_len):
    t_tokens = x.shape[0]
    tm = TOKEN_TILE
    n_steps = t_tokens // tm
    tiles_per_seq = seq_len // tm
    halo_per_tile = tm // POOL_HALO
    pool_w, pool_idx, pool_scale = pool_params
    final = final_w is not None
    nxt = lambda i: jnp.minimum(i + 1, n_steps - 1)
    in_specs = [
        pl.BlockSpec((tm, D_MODEL), lambda i: (i, 0)),
        pl.BlockSpec((tm, D_MODEL), lambda i: (nxt(i), 0)),
        pl.BlockSpec((POOL_HALO, D_MODEL), lambda i: (jnp.minimum(i + 2, n_steps - 1) * halo_per_tile, 0)),
        pl.BlockSpec((None, N_MOD, D_MODEL), lambda i: (i // tiles_per_seq, 0, 0)),
        pl.BlockSpec((None, N_MOD, D_MODEL), lambda i: (nxt(i) // tiles_per_seq, 0, 0)),
        _resident((1, D_MODEL)),
        _resident((1, D_MODEL)),
        _layer_resident(pool_w, pool_idx),
        _resident((1, D_MODEL)),
        _layer_resident(ffn_w_in, layer),
        _layer_resident(ffn_w_out, layer),
    ]
    args = [x, x, x, mod, mod, n1w.reshape(1, D_MODEL), n2w.reshape(1, D_MODEL), pool_w,
            pool_scale.reshape(1, D_MODEL), ffn_w_in, ffn_w_out]
    if final:
        in_specs.append(_resident((1, D_MODEL)))
        args.append(final_w.reshape(1, D_MODEL))
    return pl.pallas_call(
        functools.partial(_pool_ffn_kernel, final=final, seq_len=seq_len),
        grid=(n_steps,),
        in_specs=in_specs,
        out_specs=pl.BlockSpec((tm, D_MODEL), lambda i: (i, 0)),
        out_shape=jax.ShapeDtypeStruct((t_tokens, D_MODEL), F32),
        scratch_shapes=[pltpu.VMEM((tm, D_MODEL), BF16),
                        pltpu.VMEM((tm + 2 * POOL_HALO, D_MODEL), F32)],
        compiler_params=pltpu.CompilerParams(
            dimension_semantics=("arbitrary",), vmem_limit_bytes=VMEM_LIMIT_BYTES),
        name="pool_ffn",
    )(*args)


def _rope_tables(seq_len):
    rows = seq_len // GRID_W
    row = jnp.repeat(jnp.arange(rows), GRID_W).astype(F32)
    col = jnp.tile(jnp.arange(GRID_W), rows).astype(F32)
    d = RET_QK_DIM // 2
    inv = ROPE_THETA ** (-jnp.arange(0, d, 2, dtype=F32) / d)
    ang_r = row[:, None] * inv[None, :]
    ang_c = col[:, None] * inv[None, :]
    cos = jnp.concatenate([jnp.cos(ang_r), jnp.cos(ang_r), jnp.cos(ang_c), jnp.cos(ang_c)], axis=-1)
    sin = jnp.concatenate([-jnp.sin(ang_r), jnp.sin(ang_r), -jnp.sin(ang_c), jnp.sin(ang_c)], axis=-1)
    return cos, sin


def kernel(x, c, ctx, c_ctx, ada_w, ada_b, norm1_w, norm2_w, ret_w_in, ret_log_decay, ret_gn_w,
           ret_w_out, pool_w, pool_scale, ffn_w_in, ffn_w_out, final_norm_w):
    bsz, seq_len, _ = x.shape
    ctx_len = ctx.shape[1]
    depth = ada_w.shape[0]

    pad = (-(bsz + 1)) % 8
    c_all = jnp.concatenate([c, c_ctx[None, :], jnp.zeros((pad, D_MODEL), F32)], axis=0)
    mods = _mods(c_all, ada_w, ada_b).reshape(depth, bsz + 1 + pad, N_MOD, D_MODEL)

    rope_tabs = _rope_tables(seq_len)
    ret_w_in_b = ret_w_in.astype(BF16)
    ret_w_out_b = ret_w_out.astype(BF16)
    pool_w_b = pool_w.astype(BF16)
    ffn_w_in_b = ffn_w_in.astype(BF16)
    ffn_w_out_b = ffn_w_out.astype(BF16)

    x_lat = x.reshape(bsz * seq_len, D_MODEL)
    x_ctx = ctx.reshape(bsz * ctx_len, D_MODEL)
    for i in range(depth):
        last = i == depth - 1
        use_ret = (i % N_MIXERS) == 0
        j = i // N_MIXERS
        if last and use_ret:
            raise NotImplementedError("a final retention layer (context-state-only path) is not built")
        mod_lat = mods[i, :bsz]
        mod_ctx = mods[i, bsz:bsz + 1]
        final_w = final_norm_w if last else None
        if use_ret:
            lat_arrays = _ret_proj(x_lat, mod_lat, norm1_w[i], ret_w_in_b, j, ret_log_decay[j],
                                   rope_tabs, seq_len)
            ctx_arrays = _ret_proj(x_ctx, mod_ctx, norm1_w[i], ret_w_in_b, j, ret_log_decay[j],
                                   None, ctx_len)
            o_ctx, o_lat = _retention(
                ret_log_decay[j], ret_gn_w[j], ret_w_out_b, j,
                [a.reshape(bsz, ctx_len, a.shape[-1]) for a in ctx_arrays],
                [a.reshape(bsz, seq_len, a.shape[-1]) for a in lat_arrays])
            x_lat = _mix_ffn(o_lat.reshape(bsz * seq_len, D_MODEL), x_lat, mod_lat, norm2_w[i], None,
                             ffn_w_in_b, ffn_w_out_b, i, final_w, seq_len)
            x_ctx = _mix_ffn(o_ctx.reshape(bsz * ctx_len, D_MODEL), x_ctx, mod_ctx, norm2_w[i], None,
                             ffn_w_in_b, ffn_w_out_b, i, None, ctx_len)
        else:
            pool_params = (pool_w_b, j, pool_scale[j])
            if not last:
                d_ctx = _pool(x_ctx, mod_ctx, norm1_w[i], ctx_len)
                x_ctx = _mix_ffn(d_ctx, x_ctx, mod_ctx, norm2_w[i], pool_params,
                                 ffn_w_in_b, ffn_w_out_b, i, None, ctx_len)
            x_lat = _pool_ffn(x_lat, mod_lat, norm1_w[i], norm2_w[i], pool_params,
                              ffn_w_in_b, ffn_w_out_b, i, final_w, seq_len)
    return x_lat.reshape(bsz, seq_len, D_MODEL)
```

```python
import functools

import jax
import jax.numpy as jnp
from jax import lax
from jax.experimental import pallas as pl
from jax.experimental.pallas import tpu as pltpu

D_MODEL = 1024
GRID_W = 64
N_MIXERS = 2
RET_HEADS = 4
RET_QK_DIM = D_MODEL // RET_HEADS
RET_V_DIM = 2 * D_MODEL // RET_HEADS
RET_QK_TOTAL = RET_HEADS * RET_QK_DIM
RET_V_TOTAL = RET_HEADS * RET_V_DIM
RET_IN_TOTAL = 2 * RET_QK_TOTAL + 2 * RET_V_TOTAL
POOL_WINDOWS = (2, 4, 8, 16)
POOL_GROUP = D_MODEL // len(POOL_WINDOWS)
POOL_HALO = 8
ROPE_THETA = 10000.0
EPS = 1e-6
GN_EPS = 1e-5
N_MOD = 6

F32 = jnp.float32
BF16 = jnp.bfloat16

VMEM_LIMIT_BYTES = 56 * 1024 * 1024
TOKEN_TILE = 512
SUB_TILE = 256
POOL_TILE = 256
POOL_NORM_ROWS = 8
POOL_SUM_ROWS = 128
LANES = 128
RET_CHUNK = 256
MODS_COL_TILE = 1536


def _dot(a, b):
    return jnp.dot(a, b, preferred_element_type=F32)


def _rmsnorm(x, w):
    return x * lax.rsqrt(jnp.mean(x * x, axis=-1, keepdims=True) + EPS) * w


def _norm_mod(x, w, scale, shift):
    return _rmsnorm(x, w * (1.0 + scale)) + shift


def _resident(shape):
    nd = len(shape)
    return pl.BlockSpec(shape, lambda *_: (0,) * nd, pipeline_mode=pl.Buffered(1))


def _layer_resident(stacked, layer):
    nd = stacked.ndim - 1
    return pl.BlockSpec((None,) + stacked.shape[1:], lambda *_: (layer,) + (0,) * nd,
                        pipeline_mode=pl.Buffered(1))


def _mods_kernel(c_ref, w_ref, b_ref, o_ref):
    s = jax.nn.silu(c_ref[...]).astype(BF16)
    o_ref[...] = _dot(s, w_ref[...].astype(BF16)) + b_ref[...]


def _mods(c_all, ada_w, ada_b):
    depth, _, n = ada_w.shape
    r = c_all.shape[0]
    tn = MODS_COL_TILE
    return pl.pallas_call(
        _mods_kernel,
        grid=(depth, n // tn),
        in_specs=[
            pl.BlockSpec((r, D_MODEL), lambda l, j: (0, 0)),
            pl.BlockSpec((None, D_MODEL, tn), lambda l, j: (l, 0, j)),
            pl.BlockSpec((None, 1, tn), lambda l, j: (l, 0, j)),
        ],
        out_specs=pl.BlockSpec((None, r, tn), lambda l, j: (l, 0, j)),
        out_shape=jax.ShapeDtypeStruct((depth, r, n), F32),
        compiler_params=pltpu.CompilerParams(
            dimension_semantics=("arbitrary", "arbitrary"), vmem_limit_bytes=VMEM_LIMIT_BYTES),
        name="adaln_mods",
    )(c_all, ada_w, ada_b.reshape(depth, 1, n))


def _rope_half(t, cos, sin):
    return t * cos + pltpu.roll(t, 64, axis=1) * sin


def _ret_proj_kernel(*refs, rope):
    if rope:
        x_ref, mod_ref, nw_ref, w_ref, cos_ref, sin_ref, q_ref, k_ref, v_ref, g_ref = refs
    else:
        x_ref, mod_ref, nw_ref, w_ref, q_ref, k_ref, v_ref, g_ref = refs
    q_scale = RET_QK_DIM ** -0.5
    half = RET_QK_DIM // 2
    v_lo = 2 * RET_QK_TOTAL

    def norm1(rows):
        return _norm_mod(x_ref[rows, :], nw_ref[...], mod_ref[1:2, :], mod_ref[0:1, :]).astype(BF16)

    def qk_proj(rows, hb, out_ref, base, scale):
        for hd in range(RET_HEADS):
            lo = hd * RET_QK_DIM
            t = _dot(hb, w_ref[:, base + lo:base + lo + RET_QK_DIM])
            if rope:
                t = jnp.concatenate(
                    [_rope_half(t[:, :half], cos_ref[rows, :half], sin_ref[rows, :half]),
                     _rope_half(t[:, half:], cos_ref[rows, half:], sin_ref[rows, half:])], axis=-1)
            if scale is not None:
                t = t * scale
            out_ref[rows, lo:lo + RET_QK_DIM] = t.astype(BF16)

    def vg_proj(rows, hb):
        v_ref[rows, :] = _dot(hb, w_ref[:, v_lo:v_lo + RET_V_TOTAL]).astype(BF16)
        g_ref[rows, :] = _dot(hb, w_ref[:, v_lo + RET_V_TOTAL:]).astype(BF16)

    n_sub = x_ref.shape[0] // SUB_TILE
    rows = [pl.ds(s * SUB_TILE, SUB_TILE) for s in range(n_sub)]
    hb = norm1(rows[0])
    for s in range(n_sub):
        qk_proj(rows[s], hb, q_ref, 0, q_scale)
        hb_next = norm1(rows[s + 1]) if s + 1 < n_sub else None
        qk_proj(rows[s], hb, k_ref, RET_QK_TOTAL, None)
        vg_proj(rows[s], hb)
        hb = hb_next


def _ret_proj(x, mod, nw, w_in, layer, rope_tabs, seq_len):
    t_tokens = x.shape[0]
    tm = TOKEN_TILE
    tiles_per_seq = seq_len // tm
    per_batch = mod.shape[0] > 1
    mod_map = (lambda i: (i // tiles_per_seq, 0, 0)) if per_batch else (lambda i: (0, 0, 0))
    tok = lambda i: (i, 0)
    in_specs = [
        pl.BlockSpec((tm, D_MODEL), tok),
        pl.BlockSpec((None, N_MOD, D_MODEL), mod_map),
        _resident((1, D_MODEL)),
        _layer_resident(w_in, layer),
    ]
    args = [x, mod, nw.reshape(1, D_MODEL), w_in]
    rope = rope_tabs is not None
    if rope:
        pos = lambda i: (i % tiles_per_seq, 0)
        in_specs += [pl.BlockSpec((tm, RET_QK_DIM), pos), pl.BlockSpec((tm, RET_QK_DIM), pos)]
        args += list(rope_tabs)
    return pl.pallas_call(
        functools.partial(_ret_proj_kernel, rope=rope),
        grid=(t_tokens // tm,),
        in_specs=in_specs,
        out_specs=[
            pl.BlockSpec((tm, RET_QK_TOTAL), tok),
            pl.BlockSpec((tm, RET_QK_TOTAL), tok),
            pl.BlockSpec((tm, RET_V_TOTAL), tok),
            pl.BlockSpec((tm, RET_V_TOTAL), tok),
        ],
        out_shape=[
            jax.ShapeDtypeStruct((t_tokens, RET_QK_TOTAL), BF16),
            jax.ShapeDtypeStruct((t_tokens, RET_QK_TOTAL), BF16),
            jax.ShapeDtypeStruct((t_tokens, RET_V_TOTAL), BF16),
            jax.ShapeDtypeStruct((t_tokens, RET_V_TOTAL), BF16),
        ],
        compiler_params=pltpu.CompilerParams(
            dimension_semantics=("arbitrary",), vmem_limit_bytes=VMEM_LIMIT_BYTES),
        name="ret_proj",
    )(*args)


def _retention_kernel(lg_ref, gnw_ref, wout_ref, qc_ref, kc_ref, vc_ref, gc_ref, ql_ref, kl_ref, vl_ref,
                      gl_ref, oc_ref, ol_ref, y_scr, *, n_ctx, n_lat):
    c = RET_CHUNK
    hd = pl.program_id(1)
    lg_f = -jnp.abs(lg_ref[0, hd])
    lg_b = -jnp.abs(lg_ref[1, hd])
    ii = lax.broadcasted_iota(jnp.int32, (c, c), 0)
    jj = lax.broadcasted_iota(jnp.int32, (c, c), 1)
    diff = (ii - jj).astype(F32)
    intra = jnp.exp(jnp.where(diff >= 0, lg_f * diff, lg_b * (-diff)))
    idx = lax.broadcasted_iota(jnp.int32, (c, 1), 0).astype(F32)
    idx_row = lax.broadcasted_iota(jnp.int32, (1, c), 1).astype(F32)
    qd_f = jnp.exp(lg_f * (idx + 1.0))
    kd_f = jnp.exp(lg_f * (c - 1.0 - idx_row))
    qd_b = jnp.exp(lg_b * (c - idx))
    kd_b = jnp.exp(lg_b * idx_row)
    cd_f = jnp.exp(jnp.full((1, 1), lg_f * c, F32))
    cd_b = jnp.exp(jnp.full((1, 1), lg_b * c, F32))
    gnw = gnw_ref[...]

    @pl.when(hd == 0)
    def _():
        oc_ref[...] = jnp.zeros_like(oc_ref)
        ol_ref[...] = jnp.zeros_like(ol_ref)

    def chunk_refs(kind, j):
        rows = pl.ds(j * c, c)
        if kind == "c":
            return qc_ref.at[rows], kc_ref.at[rows], vc_ref.at[rows], gc_ref.at[rows], oc_ref.at[rows]
        return ql_ref.at[rows], kl_ref.at[rows], vl_ref.at[rows], gl_ref.at[rows], ol_ref.at[rows]

    def inter(q_r, k_r, v_r, state, qd, kd, cd):
        kdt = (k_r[...].astype(F32).T * kd).astype(BF16)
        upd = _dot(kdt, v_r[...])
        if state is None:
            return None, upd
        out = _dot((q_r[...].astype(F32) * qd).astype(BF16), state.astype(BF16))
        return out, state * cd + upd

    def intra_chunk(q_r, k_r, v_r):
        scores = lax.dot_general(q_r[...], k_r[...], (((1,), (1,)), ((), ())), preferred_element_type=F32)
        return _dot((scores * intra).astype(BF16), v_r[...])

    def finalize(y, g_r, o_r):
        mu = jnp.mean(y, axis=-1, keepdims=True)
        yc = y - mu
        var = jnp.mean(yc * yc, axis=-1, keepdims=True)
        yn = yc * lax.rsqrt(var + GN_EPS) * gnw
        z = (jax.nn.silu(g_r[...].astype(F32)) * yn).astype(BF16)
        o_r[...] += _dot(z, wout_ref[...])

    fwd_order = [("c", j) for j in range(n_ctx)] + [("l", j) for j in range(n_lat)]
    bwd_order = [("c", j) for j in reversed(range(n_ctx))] + [("l", j) for j in reversed(range(n_lat))]
    offset = {"c": 0, "l": n_ctx * c}
    first_visit = {}
    deferred = []
    s_f = s_b = None
    for key_f, key_b in zip(fwd_order, bwd_order):
        q_r, k_r, v_r, g_f, o_f = chunk_refs(*key_f)
        out_f, s_f = inter(q_r, k_r, v_r, s_f, qd_f, kd_f, cd_f)
        out_i = intra_chunk(q_r, k_r, v_r)
        out_f = out_i if out_f is None else out_i + out_f
        q_r, k_r, v_r, g_b, o_b = chunk_refs(*key_b)
        out_b, s_b = inter(q_r, k_r, v_r, s_b, qd_b, kd_b, cd_b)
        for y, g_r, o_r in deferred:
            finalize(y, g_r, o_r)
        deferred = []
        if key_f == key_b:
            deferred.append((out_f if out_b is None else out_f + out_b, g_f, o_f))
            continue
        for key, out, g_r, o_r in ((key_f, out_f, g_f, o_f), (key_b, out_b, g_b, o_b)):
            rows = pl.ds(offset[key[0]] + key[1] * c, c)
            if key not in first_visit:
                if out is not None:
                    y_scr[rows, :] = out
                first_visit[key] = out is not None
            elif not first_visit[key]:
                deferred.append((out, g_r, o_r))
            else:
                deferred.append((y_scr[rows, :] if out is None else y_scr[rows, :] + out, g_r, o_r))
    for y, g_r, o_r in deferred:
        finalize(y, g_r, o_r)


def _retention(log_decay, gn_w, w_out, layer, qc, kc, vc, gc, ql, kl, vl, gl):
    bsz, ctx_len, _ = qc.shape
    seq_len = ql.shape[1]
    n_ctx, n_lat = ctx_len // RET_CHUNK, seq_len // RET_CHUNK
    bh = lambda b, h: (b, 0, h)
    qk_c = pl.BlockSpec((None, ctx_len, RET_QK_DIM), bh)
    v_c = pl.BlockSpec((None, ctx_len, RET_V_DIM), bh)
    qk_l = pl.BlockSpec((None, seq_len, RET_QK_DIM), bh)
    v_l = pl.BlockSpec((None, seq_len, RET_V_DIM), bh)
    return pl.pallas_call(
        functools.partial(_retention_kernel, n_ctx=n_ctx, n_lat=n_lat),
        grid=(bsz, RET_HEADS),
        in_specs=[
            pl.BlockSpec(memory_space=pltpu.SMEM),
            pl.BlockSpec((1, RET_V_DIM), lambda b, h: (0, h)),
            pl.BlockSpec((None, RET_V_DIM, D_MODEL), lambda b, h: (layer, h, 0)),
            qk_c, qk_c, v_c, v_c, qk_l, qk_l, v_l, v_l,
        ],
        out_specs=[
            pl.BlockSpec((None, ctx_len, D_MODEL), lambda b, h: (b, 0, 0)),
            pl.BlockSpec((None, seq_len, D_MODEL), lambda b, h: (b, 0, 0)),
        ],
        out_shape=[
            jax.ShapeDtypeStruct((bsz, ctx_len, D_MODEL), F32),
            jax.ShapeDtypeStruct((bsz, seq_len, D_MODEL), F32),
        ],
        scratch_shapes=[pltpu.VMEM((ctx_len + seq_len, RET_V_DIM), F32)],
        compiler_params=pltpu.CompilerParams(
            dimension_semantics=("arbitrary", "arbitrary"), vmem_limit_bytes=VMEM_LIMIT_BYTES),
        name="retention",
    )(log_decay, gn_w.reshape(1, RET_V_TOTAL), w_out, qc, kc, vc, gc, ql, kl, vl, gl)


def _pool_tile(x_ref, x_prev, x_next, has_prev, has_next, gain, shift, inv_ref, h_scr, d_ref):
    tm = x_ref.shape[0]
    hal = POOL_HALO
    h_scr[0:hal, :] = jnp.where(has_prev, _rmsnorm(x_prev, gain) + shift, 0.0)
    for r in range(0, tm, POOL_NORM_ROWS):
        h_scr[hal + r:hal + r + POOL_NORM_ROWS, :] = _rmsnorm(x_ref[r:r + POOL_NORM_ROWS, :], gain) + shift
    h_scr[hal + tm:, :] = jnp.where(has_next, _rmsnorm(x_next, gain) + shift, 0.0)

    pb = min(POOL_SUM_ROWS, tm)
    n_ext = pb + 2 * hal

    def shifted(a, k):
        return pltpu.roll(a, k % n_ext, axis=0)

    for r in range(0, tm, pb):
        for gi, w in enumerate(POOL_WINDOWS):
            assert w // 2 <= hal
            inv_cnt = inv_ref[gi, r:r + pb, :]
            for lo in range(gi * POOL_GROUP, (gi + 1) * POOL_GROUP, LANES):
                e = h_scr[r:r + n_ext, lo:lo + LANES]
                c = e
                a = 1
                while a < w:
                    c = c + shifted(c, a)
                    a *= 2
                s = shifted(c, -(w // 2 - 1)) if w > 2 else c
                d_ref[r:r + pb, lo:lo + LANES] = (
                    s[hal:hal + pb, :] * inv_cnt - e[hal:hal + pb, :]).astype(BF16)


def _pool_kernel(x_ref, xp_ref, xn_ref, mod_ref, nw_ref, inv_ref, d_ref, h_scr):
    i = pl.program_id(1)
    gain = nw_ref[...] * (1.0 + mod_ref[1:2, :])
    _pool_tile(x_ref, xp_ref[...], xn_ref[...], i > 0, i < pl.num_programs(1) - 1,
               gain, mod_ref[0:1, :], inv_ref, h_scr, d_ref)


def _pool_inv_counts(seq_len):
    t = jnp.arange(seq_len)
    inv = [1.0 / (jnp.minimum(t + w // 2, seq_len) - jnp.maximum(t - w // 2, 0)).astype(F32)
           for w in POOL_WINDOWS]
    return jnp.broadcast_to(jnp.stack(inv)[:, :, None], (len(POOL_WINDOWS), seq_len, LANES))


def _pool(x, mod, nw, seq_len):
    t_tokens = x.shape[0]
    bsz = t_tokens // seq_len
    tm = min(POOL_TILE, seq_len)
    n_tiles = seq_len // tm
    halo_per_tile = tm // POOL_HALO
    n_halo = seq_len // POOL_HALO
    per_batch = mod.shape[0] > 1
    x3 = x.reshape(bsz, seq_len, D_MODEL)
    n_groups = len(POOL_WINDOWS)
    out = pl.pallas_call(
        _pool_kernel,
        grid=(bsz, n_tiles),
        in_specs=[
            pl.BlockSpec((None, tm, D_MODEL), lambda b, i: (b, i, 0)),
            pl.BlockSpec((None, POOL_HALO, D_MODEL),
                         lambda b, i: (b, jnp.maximum(i * halo_per_tile - 1, 0), 0)),
            pl.BlockSpec((None, POOL_HALO, D_MODEL),
                         lambda b, i: (b, jnp.minimum((i + 1) * halo_per_tile, n_halo - 1), 0)),
            pl.BlockSpec((None, N_MOD, D_MODEL),
                         (lambda b, i: (b, 0, 0)) if per_batch else (lambda b, i: (0, 0, 0))),
            _resident((1, D_MODEL)),
            pl.BlockSpec((n_groups, tm, LANES), lambda b, i: (0, i, 0)),
        ],
        out_specs=pl.BlockSpec((None, tm, D_MODEL), lambda b, i: (b, i, 0)),
        out_shape=jax.ShapeDtypeStruct((bsz, seq_len, D_MODEL), BF16),
        scratch_shapes=[pltpu.VMEM((tm + 2 * POOL_HALO, D_MODEL), F32)],
        compiler_params=pltpu.CompilerParams(
            dimension_semantics=("arbitrary", "arbitrary"), vmem_limit_bytes=VMEM_LIMIT_BYTES),
        name="pool_mix",
    )(x3, x3, x3, mod, nw.reshape(1, D_MODEL), _pool_inv_counts(seq_len))
    return out.reshape(t_tokens, D_MODEL)


def _pool_mix(d, wmix_ref, ps_ref):
    return jnp.concatenate(
        [_dot(d[:, g * POOL_GROUP:(g + 1) * POOL_GROUP], wmix_ref[g]) for g in range(len(POOL_WINDOWS))],
        axis=-1) * ps_ref[...]


def _ffn_pipeline(mix, x_ref, mod_ref, n2w_ref, win_ref, wout_ref, fnw_ref, o_ref, between=None):
    hidden = win_ref.shape[-1] // 2

    def norm2(rows, y):
        x1 = x_ref[rows, :] + mod_ref[2:3, :] * y
        h2 = _norm_mod(x1, n2w_ref[...], mod_ref[4:5, :], mod_ref[3:4, :]).astype(BF16)
        return x1, h2

    def gate(ab):
        return (jax.nn.silu(ab[:, :hidden]) * ab[:, hidden:]).astype(BF16)

    def finish(rows, x1, f):
        x2 = x1 + mod_ref[5:6, :] * f
        if fnw_ref is not None:
            x2 = _rmsnorm(x2, fnw_ref[...])
        o_ref[rows, :] = x2

    assert x_ref.shape[0] == 2 * SUB_TILE
    ra, rb = (pl.ds(s * SUB_TILE, SUB_TILE) for s in range(2))
    y_a = mix(ra)
    y_b = mix(rb)
    x1_a, h2_a = norm2(ra, y_a)
    ab_a = _dot(h2_a, win_ref[...])
    if between is not None:
        between()
    x1_b, h2_b = norm2(rb, y_b)
    ab_b = _dot(h2_b, win_ref[...])
    f_a = _dot(gate(ab_a), wout_ref[...])
    f_b = _dot(gate(ab_b), wout_ref[...])
    finish(ra, x1_a, f_a)
    finish(rb, x1_b, f_b)


def _mix_ffn_kernel(*refs, pool, final):
    refs = list(refs)
    z_ref, x_ref, mod_ref, n2w_ref = refs[:4]
    rest = refs[4:]
    wmix_ref, ps_ref = (rest.pop(0), rest.pop(0)) if pool else (None, None)
    win_ref, wout_ref = rest.pop(0), rest.pop(0)
    fnw_ref = rest.pop(0) if final else None
    o_ref = rest.pop(0)

    def mix(rows):
        if pool:
            return _pool_mix(z_ref[rows, :], wmix_ref, ps_ref)
        return z_ref[rows, :]

    _ffn_pipeline(mix, x_ref, mod_ref, n2w_ref, win_ref, wout_ref, fnw_ref, o_ref)


def _mix_ffn(z, x, mod, n2w, pool_params, ffn_w_in, ffn_w_out, layer, final_w, seq_len):
    t_tokens = x.shape[0]
    tm = TOKEN_TILE
    tiles_per_seq = seq_len // tm
    per_batch = mod.shape[0] > 1
    mod_map = (lambda i: (i // tiles_per_seq, 0, 0)) if per_batch else (lambda i: (0, 0, 0))
    tok = lambda i: (i, 0)
    pool = pool_params is not None
    final = final_w is not None
    in_specs = [
        pl.BlockSpec((tm, z.shape[1]), tok),
        pl.BlockSpec((tm, D_MODEL), tok),
        pl.BlockSpec((None, N_MOD, D_MODEL), mod_map),
        _resident((1, D_MODEL)),
    ]
    args = [z, x, mod, n2w.reshape(1, D_MODEL)]
    if pool:
        pool_w, pool_idx, pool_scale = pool_params
        in_specs += [_layer_resident(pool_w, pool_idx), _resident((1, D_MODEL))]
        args += [pool_w, pool_scale.reshape(1, D_MODEL)]
    in_specs += [_layer_resident(ffn_w_in, layer), _layer_resident(ffn_w_out, layer)]
    args += [ffn_w_in, ffn_w_out]
    if final:
        in_specs.append(_resident((1, D_MODEL)))
        args.append(final_w.reshape(1, D_MODEL))
    return pl.pallas_call(
        functools.partial(_mix_ffn_kernel, pool=pool, final=final),
        grid=(t_tokens // tm,),
        in_specs=in_specs,
        out_specs=pl.BlockSpec((tm, D_MODEL), tok),
        out_shape=jax.ShapeDtypeStruct((t_tokens, D_MODEL), F32),
        compiler_params=pltpu.CompilerParams(
            dimension_semantics=("arbitrary",), vmem_limit_bytes=VMEM_LIMIT_BYTES),
        name="mix_ffn_pool" if pool else "mix_ffn_ret",
    )(*args)


def _pool_ffn_kernel(*refs, final, tiles_per_seq):
    refs = list(refs)
    (xc_ref, xn_ref, xh_ref, mod_ref, modn_ref, inv0_ref, invn_ref, n1w_ref, n2w_ref, wmix_ref, ps_ref,
     win_ref, wout_ref) = refs[:13]
    rest = refs[13:]
    fnw_ref = rest.pop(0) if final else None
    o_ref, d_scr, h_scr = rest
    tm = xc_ref.shape[0]
    hal = POOL_HALO
    i = pl.program_id(0)

    @pl.when(i == 0)
    def _():
        gain = n1w_ref[...] * (1.0 + mod_ref[1:2, :])
        _pool_tile(xc_ref, xc_ref[0:hal, :], xn_ref[0:hal, :], False, tiles_per_seq > 1,
                   gain, mod_ref[0:1, :], inv0_ref, h_scr, d_scr)

    def pool_next():
        j = (i + 1) % tiles_per_seq
        gain = n1w_ref[...] * (1.0 + modn_ref[1:2, :])
        _pool_tile(xn_ref, xc_ref[tm - hal:tm, :], xh_ref[...], j > 0, j < tiles_per_seq - 1,
                   gain, modn_ref[0:1, :], invn_ref, h_scr, d_scr)

    _ffn_pipeline(lambda rows: _pool_mix(d_scr[rows, :], wmix_ref, ps_ref),
                  xc_ref, mod_ref, n2w_ref, win_ref, wout_ref, fnw_ref, o_ref, between=pool_next)


def _pool_ffn(x, mod, n1w, n2w, pool_params, ffn_w_in, ffn_w_out, layer, final_w, seq_len):
    t_tokens = x.shape[0]
    tm = TOKEN_TILE
    n_steps = t_tokens // tm
    tiles_per_seq = seq_len // tm
    halo_per_tile = tm // POOL_HALO
    pool_w, pool_idx, pool_scale = pool_params
    final = final_w is not None
    n_groups = len(POOL_WINDOWS)
    nxt = lambda i: jnp.minimum(i + 1, n_steps - 1)
    in_specs = [
        pl.BlockSpec((tm, D_MODEL), lambda i: (i, 0)),
        pl.BlockSpec((tm, D_MODEL), lambda i: (nxt(i), 0)),
        pl.BlockSpec((POOL_HALO, D_MODEL), lambda i: (jnp.minimum(i + 2, n_steps - 1) * halo_per_tile, 0)),
        pl.BlockSpec((None, N_MOD, D_MODEL), lambda i: (i // tiles_per_seq, 0, 0)),
        pl.BlockSpec((None, N_MOD, D_MODEL), lambda i: (nxt(i) // tiles_per_seq, 0, 0)),
        _resident((n_groups, tm, LANES)),
        pl.BlockSpec((n_groups, tm, LANES), lambda i: (0, nxt(i) % tiles_per_seq, 0)),
        _resident((1, D_MODEL)),
        _resident((1, D_MODEL)),
        _layer_resident(pool_w, pool_idx),
        _resident((1, D_MODEL)),
        _layer_resident(ffn_w_in, layer),
        _layer_resident(ffn_w_out, layer),
    ]
    inv = _pool_inv_counts(seq_len)
    args = [x, x, x, mod, mod, inv, inv, n1w.reshape(1, D_MODEL), n2w.reshape(1, D_MODEL), pool_w,
            pool_scale.reshape(1, D_MODEL), ffn_w_in, ffn_w_out]
    if final:
        in_specs.append(_resident((1, D_MODEL)))
        args.append(final_w.reshape(1, D_MODEL))
    return pl.pallas_call(
        functools.partial(_pool_ffn_kernel, final=final, tiles_per_seq=tiles_per_seq),
        grid=(n_steps,),
        in_specs=in_specs,
        out_specs=pl.BlockSpec((tm, D_MODEL), lambda i: (i, 0)),
        out_shape=jax.ShapeDtypeStruct((t_tokens, D_MODEL), F32),
        scratch_shapes=[pltpu.VMEM((tm, D_MODEL), BF16),
                        pltpu.VMEM((tm + 2 * POOL_HALO, D_MODEL), F32)],
        compiler_params=pltpu.CompilerParams(
            dimension_semantics=("arbitrary",), vmem_limit_bytes=VMEM_LIMIT_BYTES),
        name="pool_ffn",
    )(*args)


def _rope_tables(seq_len):
    rows = seq_len // GRID_W
    row = jnp.repeat(jnp.arange(rows), GRID_W).astype(F32)
    col = jnp.tile(jnp.arange(GRID_W), rows).astype(F32)
    d = RET_QK_DIM // 2
    inv = ROPE_THETA ** (-jnp.arange(0, d, 2, dtype=F32) / d)
    ang_r = row[:, None] * inv[None, :]
    ang_c = col[:, None] * inv[None, :]
    cos = jnp.concatenate([jnp.cos(ang_r), jnp.cos(ang_r), jnp.cos(ang_c), jnp.cos(ang_c)], axis=-1)
    sin = jnp.concatenate([-jnp.sin(ang_r), jnp.sin(ang_r), -jnp.sin(ang_c), jnp.sin(ang_c)], axis=-1)
    return cos, sin


def kernel(x, c, ctx, c_ctx, ada_w, ada_b, norm1_w, norm2_w, ret_w_in, ret_log_decay, ret_gn_w,
           ret_w_out, pool_w, pool_scale, ffn_w_in, ffn_w_out, final_norm_w):
    bsz, seq_len, _ = x.shape
    ctx_len = ctx.shape[1]
    depth = ada_w.shape[0]

    pad = (-(bsz + 1)) % 8
    c_all = jnp.concatenate([c, c_ctx[None, :], jnp.zeros((pad, D_MODEL), F32)], axis=0)
    mods = _mods(c_all, ada_w, ada_b).reshape(depth, bsz + 1 + pad, N_MOD, D_MODEL)

    rope_tabs = _rope_tables(seq_len)
    ret_w_in_b = ret_w_in.astype(BF16)
    ret_w_out_b = ret_w_out.astype(BF16)
    pool_w_b = pool_w.astype(BF16)
    ffn_w_in_b = ffn_w_in.astype(BF16)
    ffn_w_out_b = ffn_w_out.astype(BF16)

    x_lat = x.reshape(bsz * seq_len, D_MODEL)
    x_ctx = ctx.reshape(bsz * ctx_len, D_MODEL)
    for i in range(depth):
        last = i == depth - 1
        use_ret = (i % N_MIXERS) == 0
        j = i // N_MIXERS
        if last and use_ret:
            raise NotImplementedError("a final retention layer (context-state-only path) is not built")
        mod_lat = mods[i, :bsz]
        mod_ctx = mods[i, bsz:bsz + 1]
        final_w = final_norm_w if last else None
        if use_ret:
            ql, kl, vl, gl = _ret_proj(x_lat, mod_lat, norm1_w[i], ret_w_in_b, j, rope_tabs, seq_len)
            qc, kc, vc, gc = _ret_proj(x_ctx, mod_ctx, norm1_w[i], ret_w_in_b, j, None, ctx_len)
            r3 = lambda a, n: a.reshape(bsz, n, a.shape[-1])
            o_ctx, o_lat = _retention(
                ret_log_decay[j], ret_gn_w[j], ret_w_out_b, j,
                r3(qc, ctx_len), r3(kc, ctx_len), r3(vc, ctx_len), r3(gc, ctx_len),
                r3(ql, seq_len), r3(kl, seq_len), r3(vl, seq_len), r3(gl, seq_len))
            x_lat = _mix_ffn(o_lat.reshape(bsz * seq_len, D_MODEL), x_lat, mod_lat, norm2_w[i], None,
                             ffn_w_in_b, ffn_w_out_b, i, final_w, seq_len)
            x_ctx = _mix_ffn(o_ctx.reshape(bsz * ctx_len, D_MODEL), x_ctx, mod_ctx, norm2_w[i], None,
                             ffn_w_in_b, ffn_w_out_b, i, None, ctx_len)
        else:
            pool_params = (pool_w_b, j, pool_scale[j])
            if not last:
                d_ctx = _pool(x_ctx, mod_ctx, norm1_w[i], ctx_len)
                x_ctx = _mix_ffn(d_ctx, x_ctx, mod_ctx, norm2_w[i], pool_params,
                                 ffn_w_in_b, ffn_w_out_b, i, None, ctx_len)
            x_lat = _pool_ffn(x_lat, mod_lat, norm1_w[i], norm2_w[i], pool_params,
                              ffn_w_in_b, ffn_w_out_b, i, final_w, seq_len)
    return x_lat.reshape(bsz, seq_len, D_MODEL)
```

```python
import functools

import jax
import jax.numpy as jnp
from jax import lax
from jax.experimental import pallas as pl
from jax.experimental.pallas import tpu as pltpu

D_MODEL = 1024
GRID_W = 64
N_MIXERS = 2
RET_HEADS = 4
RET_QK_DIM = D_MODEL // RET_HEADS
RET_V_DIM = 2 * D_MODEL // RET_HEADS
RET_QK_TOTAL = RET_HEADS * RET_QK_DIM
RET_V_TOTAL = RET_HEADS * RET_V_DIM
RET_IN_TOTAL = 2 * RET_QK_TOTAL + 2 * RET_V_TOTAL
POOL_WINDOWS = (2, 4, 8, 16)
POOL_GROUP = D_MODEL // len(POOL_WINDOWS)
POOL_HALO = 8
ROPE_THETA = 10000.0
EPS = 1e-6
GN_EPS = 1e-5
N_MOD = 6

F32 = jnp.float32
BF16 = jnp.bfloat16

VMEM_LIMIT_BYTES = 56 * 1024 * 1024
TOKEN_TILE = 512
SUB_TILE = 256
POOL_TILE = 256
POOL_NORM_ROWS = 8
POOL_SUM_ROWS = 128
LANES = 128
RET_CHUNK = 256
MODS_COL_TILE = 1536


def _dot(a, b):
    return jnp.dot(a, b, preferred_element_type=F32)


def _rmsnorm(x, w):
    return x * lax.rsqrt(jnp.mean(x * x, axis=-1, keepdims=True) + EPS) * w


def _norm_mod(x, w, scale, shift):
    return _rmsnorm(x, w * (1.0 + scale)) + shift


def _resident(shape):
    nd = len(shape)
    return pl.BlockSpec(shape, lambda *_: (0,) * nd, pipeline_mode=pl.Buffered(1))


def _layer_resident(stacked, layer):
    nd = stacked.ndim - 1
    return pl.BlockSpec((None,) + stacked.shape[1:], lambda *_: (layer,) + (0,) * nd,
                        pipeline_mode=pl.Buffered(1))


def _mods_kernel(c_ref, w_ref, b_ref, o_ref):
    s = jax.nn.silu(c_ref[...]).astype(BF16)
    o_ref[...] = _dot(s, w_ref[...].astype(BF16)) + b_ref[...]


def _mods(c_all, ada_w, ada_b):
    depth, _, n = ada_w.shape
    r = c_all.shape[0]
    tn = MODS_COL_TILE
    return pl.pallas_call(
        _mods_kernel,
        grid=(depth, n // tn),
        in_specs=[
            pl.BlockSpec((r, D_MODEL), lambda l, j: (0, 0)),
            pl.BlockSpec((None, D_MODEL, tn), lambda l, j: (l, 0, j)),
            pl.BlockSpec((None, 1, tn), lambda l, j: (l, 0, j)),
        ],
        out_specs=pl.BlockSpec((None, r, tn), lambda l, j: (l, 0, j)),
        out_shape=jax.ShapeDtypeStruct((depth, r, n), F32),
        compiler_params=pltpu.CompilerParams(
            dimension_semantics=("arbitrary", "arbitrary"), vmem_limit_bytes=VMEM_LIMIT_BYTES),
        name="adaln_mods",
    )(c_all, ada_w, ada_b.reshape(depth, 1, n))


def _rope_half(t, cos, sin):
    return t * cos + pltpu.roll(t, 64, axis=1) * sin


def _ret_proj_kernel(*refs, rope):
    if rope:
        x_ref, mod_ref, nw_ref, w_ref, cos_ref, sin_ref, q_ref, k_ref, v_ref, g_ref = refs
    else:
        x_ref, mod_ref, nw_ref, w_ref, q_ref, k_ref, v_ref, g_ref = refs
    q_scale = RET_QK_DIM ** -0.5
    half = RET_QK_DIM // 2
    v_lo = 2 * RET_QK_TOTAL

    def norm1(rows):
        return _norm_mod(x_ref[rows, :], nw_ref[...], mod_ref[1:2, :], mod_ref[0:1, :]).astype(BF16)

    def qk_proj(rows, hb, out_ref, base, scale):
        for hd in range(RET_HEADS):
            lo = hd * RET_QK_DIM
            t = _dot(hb, w_ref[:, base + lo:base + lo + RET_QK_DIM])
            if rope:
                t = jnp.concatenate(
                    [_rope_half(t[:, :half], cos_ref[rows, :half], sin_ref[rows, :half]),
                     _rope_half(t[:, half:], cos_ref[rows, half:], sin_ref[rows, half:])], axis=-1)
            if scale is not None:
                t = t * scale
            out_ref[rows, lo:lo + RET_QK_DIM] = t.astype(BF16)

    def vg_proj(rows, hb):
        v_ref[rows, :] = _dot(hb, w_ref[:, v_lo:v_lo + RET_V_TOTAL]).astype(BF16)
        g_ref[rows, :] = _dot(hb, w_ref[:, v_lo + RET_V_TOTAL:]).astype(BF16)

    n_sub = x_ref.shape[0] // SUB_TILE
    rows = [pl.ds(s * SUB_TILE, SUB_TILE) for s in range(n_sub)]
    hb = norm1(rows[0])
    for s in range(n_sub):
        qk_proj(rows[s], hb, q_ref, 0, q_scale)
        hb_next = norm1(rows[s + 1]) if s + 1 < n_sub else None
        qk_proj(rows[s], hb, k_ref, RET_QK_TOTAL, None)
        vg_proj(rows[s], hb)
        hb = hb_next


def _ret_proj(x, mod, nw, w_in, layer, rope_tabs, seq_len):
    t_tokens = x.shape[0]
    tm = TOKEN_TILE
    tiles_per_seq = seq_len // tm
    per_batch = mod.shape[0] > 1
    mod_map = (lambda i: (i // tiles_per_seq, 0, 0)) if per_batch else (lambda i: (0, 0, 0))
    tok = lambda i: (i, 0)
    in_specs = [
        pl.BlockSpec((tm, D_MODEL), tok),
        pl.BlockSpec((None, N_MOD, D_MODEL), mod_map),
        _resident((1, D_MODEL)),
        _layer_resident(w_in, layer),
    ]
    args = [x, mod, nw.reshape(1, D_MODEL), w_in]
    rope = rope_tabs is not None
    if rope:
        pos = lambda i: (i % tiles_per_seq, 0)
        in_specs += [pl.BlockSpec((tm, RET_QK_DIM), pos), pl.BlockSpec((tm, RET_QK_DIM), pos)]
        args += list(rope_tabs)
    return pl.pallas_call(
        functools.partial(_ret_proj_kernel, rope=rope),
        grid=(t_tokens // tm,),
        in_specs=in_specs,
        out_specs=[
            pl.BlockSpec((tm, RET_QK_TOTAL), tok),
            pl.BlockSpec((tm, RET_QK_TOTAL), tok),
            pl.BlockSpec((tm, RET_V_TOTAL), tok),
            pl.BlockSpec((tm, RET_V_TOTAL), tok),
        ],
        out_shape=[
            jax.ShapeDtypeStruct((t_tokens, RET_QK_TOTAL), BF16),
            jax.ShapeDtypeStruct((t_tokens, RET_QK_TOTAL), BF16),
            jax.ShapeDtypeStruct((t_tokens, RET_V_TOTAL), BF16),
            jax.ShapeDtypeStruct((t_tokens, RET_V_TOTAL), BF16),
        ],
        compiler_params=pltpu.CompilerParams(
            dimension_semantics=("arbitrary",), vmem_limit_bytes=VMEM_LIMIT_BYTES),
        name="ret_proj",
    )(*args)


def _retention_kernel(lg_ref, gnw_ref, wout_ref, qc_ref, kc_ref, vc_ref, gc_ref, ql_ref, kl_ref, vl_ref,
                      gl_ref, oc_ref, ol_ref, y_scr, *, n_ctx, n_lat):
    c = RET_CHUNK
    hd = pl.program_id(1)
    lg_f = -jnp.abs(lg_ref[0, hd])
    lg_b = -jnp.abs(lg_ref[1, hd])
    ii = lax.broadcasted_iota(jnp.int32, (c, c), 0)
    jj = lax.broadcasted_iota(jnp.int32, (c, c), 1)
    diff = (ii - jj).astype(F32)
    intra = jnp.exp(jnp.where(diff >= 0, lg_f * diff, lg_b * (-diff)))
    idx = lax.broadcasted_iota(jnp.int32, (c, 1), 0).astype(F32)
    idx_row = lax.broadcasted_iota(jnp.int32, (1, c), 1).astype(F32)
    qd_f = jnp.exp(lg_f * (idx + 1.0))
    kd_f = jnp.exp(lg_f * (c - 1.0 - idx_row))
    qd_b = jnp.exp(lg_b * (c - idx))
    kd_b = jnp.exp(lg_b * idx_row)
    cd_f = jnp.exp(jnp.full((1, 1), lg_f * c, F32))
    cd_b = jnp.exp(jnp.full((1, 1), lg_b * c, F32))
    gnw = gnw_ref[...]

    @pl.when(hd == 0)
    def _():
        oc_ref[...] = jnp.zeros_like(oc_ref)
        ol_ref[...] = jnp.zeros_like(ol_ref)

    def chunk_refs(kind, j):
        rows = pl.ds(j * c, c)
        if kind == "c":
            return qc_ref.at[rows], kc_ref.at[rows], vc_ref.at[rows], gc_ref.at[rows], oc_ref.at[rows]
        return ql_ref.at[rows], kl_ref.at[rows], vl_ref.at[rows], gl_ref.at[rows], ol_ref.at[rows]

    def inter(q_r, k_r, v_r, state, qd, kd, cd):
        kdt = (k_r[...].astype(F32).T * kd).astype(BF16)
        upd = _dot(kdt, v_r[...])
        if state is None:
            return None, upd
        out = _dot((q_r[...].astype(F32) * qd).astype(BF16), state.astype(BF16))
        return out, state * cd + upd

    def intra_chunk(q_r, k_r, v_r):
        scores = lax.dot_general(q_r[...], k_r[...], (((1,), (1,)), ((), ())), preferred_element_type=F32)
        return _dot((scores * intra).astype(BF16), v_r[...])

    def finalize(y, g_r, o_r):
        mu = jnp.mean(y, axis=-1, keepdims=True)
        yc = y - mu
        var = jnp.mean(yc * yc, axis=-1, keepdims=True)
        yn = yc * lax.rsqrt(var + GN_EPS) * gnw
        z = (jax.nn.silu(g_r[...].astype(F32)) * yn).astype(BF16)
        o_r[...] += _dot(z, wout_ref[...])

    fwd_order = [("c", j) for j in range(n_ctx)] + [("l", j) for j in range(n_lat)]
    bwd_order = [("c", j) for j in reversed(range(n_ctx))] + [("l", j) for j in reversed(range(n_lat))]
    offset = {"c": 0, "l": n_ctx * c}
    first_visit = {}
    complete = []
    s_f = s_b = None
    for key_f, key_b in zip(fwd_order, bwd_order):
        q_r, k_r, v_r, g_f, o_f = chunk_refs(*key_f)
        out_i = intra_chunk(q_r, k_r, v_r)
        qb_r, kb_r, vb_r, g_b, o_b = chunk_refs(*key_b)
        out_b, s_b = inter(qb_r, kb_r, vb_r, s_b, qd_b, kd_b, cd_b)
        out_f, s_f = inter(q_r, k_r, v_r, s_f, qd_f, kd_f, cd_f)
        out_f = out_i if out_f is None else out_i + out_f
        if key_f == key_b:
            complete.append((out_f if out_b is None else out_f + out_b, g_f, o_f))
            continue
        for key, out, g_r, o_r in ((key_f, out_f, g_f, o_f), (key_b, out_b, g_b, o_b)):
            rows = pl.ds(offset[key[0]] + key[1] * c, c)
            if key not in first_visit:
                if out is not None:
                    y_scr[rows, :] = out
                first_visit[key] = out is not None
            elif not first_visit[key]:
                complete.append((out, g_r, o_r))
            else:
                complete.append((y_scr[rows, :] if out is None else y_scr[rows, :] + out, g_r, o_r))
    assert len(complete) == n_ctx + n_lat
    for y, g_r, o_r in complete:
        finalize(y, g_r, o_r)


def _retention(log_decay, gn_w, w_out, layer, qc, kc, vc, gc, ql, kl, vl, gl):
    bsz, ctx_len, _ = qc.shape
    seq_len = ql.shape[1]
    n_ctx, n_lat = ctx_len // RET_CHUNK, seq_len // RET_CHUNK
    bh = lambda b, h: (b, 0, h)
    qk_c = pl.BlockSpec((None, ctx_len, RET_QK_DIM), bh)
    v_c = pl.BlockSpec((None, ctx_len, RET_V_DIM), bh)
    qk_l = pl.BlockSpec((None, seq_len, RET_QK_DIM), bh)
    v_l = pl.BlockSpec((None, seq_len, RET_V_DIM), bh)
    return pl.pallas_call(
        functools.partial(_retention_kernel, n_ctx=n_ctx, n_lat=n_lat),
        grid=(bsz, RET_HEADS),
        in_specs=[
            pl.BlockSpec(memory_space=pltpu.SMEM),
            pl.BlockSpec((1, RET_V_DIM), lambda b, h: (0, h)),
            pl.BlockSpec((None, RET_V_DIM, D_MODEL), lambda b, h: (layer, h, 0)),
            qk_c, qk_c, v_c, v_c, qk_l, qk_l, v_l, v_l,
        ],
        out_specs=[
            pl.BlockSpec((None, ctx_len, D_MODEL), lambda b, h: (b, 0, 0)),
            pl.BlockSpec((None, seq_len, D_MODEL), lambda b, h: (b, 0, 0)),
        ],
        out_shape=[
            jax.ShapeDtypeStruct((bsz, ctx_len, D_MODEL), F32),
            jax.ShapeDtypeStruct((bsz, seq_len, D_MODEL), F32),
        ],
        scratch_shapes=[pltpu.VMEM((ctx_len + seq_len, RET_V_DIM), F32)],
        compiler_params=pltpu.CompilerParams(
            dimension_semantics=("arbitrary", "arbitrary"), vmem_limit_bytes=VMEM_LIMIT_BYTES),
        name="retention",
    )(log_decay, gn_w.reshape(1, RET_V_TOTAL), w_out, qc, kc, vc, gc, ql, kl, vl, gl)


def _pool_tile(x_ref, x_prev, x_next, has_prev, has_next, gain, shift, inv_ref, h_scr, d_ref):
    tm = x_ref.shape[0]
    hal = POOL_HALO
    h_scr[0:hal, :] = jnp.where(has_prev, _rmsnorm(x_prev, gain) + shift, 0.0)
    for r in range(0, tm, POOL_NORM_ROWS):
        h_scr[hal + r:hal + r + POOL_NORM_ROWS, :] = _rmsnorm(x_ref[r:r + POOL_NORM_ROWS, :], gain) + shift
    h_scr[hal + tm:, :] = jnp.where(has_next, _rmsnorm(x_next, gain) + shift, 0.0)

    pb = min(POOL_SUM_ROWS, tm)
    n_ext = pb + 2 * hal

    def shifted(a, k):
        return pltpu.roll(a, k % n_ext, axis=0)

    token = jnp.zeros((8, LANES), jnp.int32)
    for r in range(0, tm, pb):
        for gi, w in enumerate(POOL_WINDOWS):
            assert w // 2 <= hal
            inv_cnt = inv_ref[gi, r:r + pb, :]
            for lo in range(gi * POOL_GROUP, (gi + 1) * POOL_GROUP, LANES):
                e = h_scr[r:r + n_ext, lo:lo + LANES]
                c = e
                a = 1
                while a < w:
                    c = c + shifted(c, a)
                    a *= 2
                s = shifted(c, -(w // 2 - 1)) if w > 2 else c
                d = s[hal:hal + pb, :] * inv_cnt - e[hal:hal + pb, :]
                d_ref[r:r + pb, lo:lo + LANES] = d.astype(BF16)
                bits = lax.bitcast_convert_type(d, jnp.int32)
                for q in range(0, pb, 8):
                    token = token | bits[q:q + 8, :]
    return token


def _pool_kernel(x_ref, xp_ref, xn_ref, mod_ref, nw_ref, inv_ref, d_ref, h_scr):
    i = pl.program_id(1)
    gain = nw_ref[...] * (1.0 + mod_ref[1:2, :])
    _pool_tile(x_ref, xp_ref[...], xn_ref[...], i > 0, i < pl.num_programs(1) - 1,
               gain, mod_ref[0:1, :], inv_ref, h_scr, d_ref)


def _pool_inv_counts(seq_len):
    t = jnp.arange(seq_len)
    inv = [1.0 / (jnp.minimum(t + w // 2, seq_len) - jnp.maximum(t - w // 2, 0)).astype(F32)
           for w in POOL_WINDOWS]
    return jnp.broadcast_to(jnp.stack(inv)[:, :, None], (len(POOL_WINDOWS), seq_len, LANES))


def _pool(x, mod, nw, seq_len):
    t_tokens = x.shape[0]
    bsz = t_tokens // seq_len
    tm = min(POOL_TILE, seq_len)
    n_tiles = seq_len // tm
    halo_per_tile = tm // POOL_HALO
    n_halo = seq_len // POOL_HALO
    per_batch = mod.shape[0] > 1
    x3 = x.reshape(bsz, seq_len, D_MODEL)
    n_groups = len(POOL_WINDOWS)
    out = pl.pallas_call(
        _pool_kernel,
        grid=(bsz, n_tiles),
        in_specs=[
            pl.BlockSpec((None, tm, D_MODEL), lambda b, i: (b, i, 0)),
            pl.BlockSpec((None, POOL_HALO, D_MODEL),
                         lambda b, i: (b, jnp.maximum(i * halo_per_tile - 1, 0), 0)),
            pl.BlockSpec((None, POOL_HALO, D_MODEL),
                         lambda b, i: (b, jnp.minimum((i + 1) * halo_per_tile, n_halo - 1), 0)),
            pl.BlockSpec((None, N_MOD, D_MODEL),
                         (lambda b, i: (b, 0, 0)) if per_batch else (lambda b, i: (0, 0, 0))),
            _resident((1, D_MODEL)),
            pl.BlockSpec((n_groups, tm, LANES), lambda b, i: (0, i, 0)),
        ],
        out_specs=pl.BlockSpec((None, tm, D_MODEL), lambda b, i: (b, i, 0)),
        out_shape=jax.ShapeDtypeStruct((bsz, seq_len, D_MODEL), BF16),
        scratch_shapes=[pltpu.VMEM((tm + 2 * POOL_HALO, D_MODEL), F32)],
        compiler_params=pltpu.CompilerParams(
            dimension_semantics=("arbitrary", "arbitrary"), vmem_limit_bytes=VMEM_LIMIT_BYTES),
        name="pool_mix",
    )(x3, x3, x3, mod, nw.reshape(1, D_MODEL), _pool_inv_counts(seq_len))
    return out.reshape(t_tokens, D_MODEL)


def _pool_mix(d, wmix_ref, ps_ref):
    return jnp.concatenate(
        [_dot(d[:, g * POOL_GROUP:(g + 1) * POOL_GROUP], wmix_ref[g]) for g in range(len(POOL_WINDOWS))],
        axis=-1) * ps_ref[...]


def _ffn_pipeline(mix, x_ref, mod_ref, n2w_ref, win_ref, wout_ref, fnw_ref, o_ref, between=None):
    hidden = win_ref.shape[-1] // 2

    def norm2(rows, y):
        x1 = x_ref[rows, :] + mod_ref[2:3, :] * y
        h2 = _norm_mod(x1, n2w_ref[...], mod_ref[4:5, :], mod_ref[3:4, :]).astype(BF16)
        return x1, h2

    def gate(ab):
        return (jax.nn.silu(ab[:, :hidden]) * ab[:, hidden:]).astype(BF16)

    def finish(rows, x1, f, gate2=None):
        x2 = x1 + (mod_ref[5:6, :] if gate2 is None else gate2) * f
        if fnw_ref is not None:
            x2 = _rmsnorm(x2, fnw_ref[...])
        o_ref[rows, :] = x2

    assert x_ref.shape[0] == 2 * SUB_TILE
    ra, rb = (pl.ds(s * SUB_TILE, SUB_TILE) for s in range(2))
    y_a = mix(ra)
    y_b = mix(rb)
    x1_a, h2_a = norm2(ra, y_a)
    ab_a = _dot(h2_a, win_ref[...])
    x1_b, h2_b = norm2(rb, y_b)
    ab_b = _dot(h2_b, win_ref[...])
    f_a = _dot(gate(ab_a), wout_ref[...])
    f_b = _dot(gate(ab_b), wout_ref[...])
    finish(ra, x1_a, f_a)
    gate2 = None
    if between is not None:
        token = between()
        zero = lax.shift_right_logical(lax.shift_right_logical(token, 16), 16)[0:1, :]
        zero_row = jnp.concatenate([zero] * (D_MODEL // LANES), axis=1)
        gate2 = lax.bitcast_convert_type(
            lax.bitcast_convert_type(mod_ref[5:6, :], jnp.int32) + zero_row, F32)
    finish(rb, x1_b, f_b, gate2)


def _mix_ffn_kernel(*refs, pool, final):
    refs = list(refs)
    z_ref, x_ref, mod_ref, n2w_ref = refs[:4]
    rest = refs[4:]
    wmix_ref, ps_ref = (rest.pop(0), rest.pop(0)) if pool else (None, None)
    win_ref, wout_ref = rest.pop(0), rest.pop(0)
    fnw_ref = rest.pop(0) if final else None
    o_ref = rest.pop(0)

    def mix(rows):
        if pool:
            return _pool_mix(z_ref[rows, :], wmix_ref, ps_ref)
        return z_ref[rows, :]

    _ffn_pipeline(mix, x_ref, mod_ref, n2w_ref, win_ref, wout_ref, fnw_ref, o_ref)


def _mix_ffn(z, x, mod, n2w, pool_params, ffn_w_in, ffn_w_out, layer, final_w, seq_len):
    t_tokens = x.shape[0]
    tm = TOKEN_TILE
    tiles_per_seq = seq_len // tm
    per_batch = mod.shape[0] > 1
    mod_map = (lambda i: (i // tiles_per_seq, 0, 0)) if per_batch else (lambda i: (0, 0, 0))
    tok = lambda i: (i, 0)
    pool = pool_params is not None
    final = final_w is not None
    in_specs = [
        pl.BlockSpec((tm, z.shape[1]), tok),
        pl.BlockSpec((tm, D_MODEL), tok),
        pl.BlockSpec((None, N_MOD, D_MODEL), mod_map),
        _resident((1, D_MODEL)),
    ]
    args = [z, x, mod, n2w.reshape(1, D_MODEL)]
    if pool:
        pool_w, pool_idx, pool_scale = pool_params
        in_specs += [_layer_resident(pool_w, pool_idx), _resident((1, D_MODEL))]
        args += [pool_w, pool_scale.reshape(1, D_MODEL)]
    in_specs += [_layer_resident(ffn_w_in, layer), _layer_resident(ffn_w_out, layer)]
    args += [ffn_w_in, ffn_w_out]
    if final:
        in_specs.append(_resident((1, D_MODEL)))
        args.append(final_w.reshape(1, D_MODEL))
    return pl.pallas_call(
        functools.partial(_mix_ffn_kernel, pool=pool, final=final),
        grid=(t_tokens // tm,),
        in_specs=in_specs,
        out_specs=pl.BlockSpec((tm, D_MODEL), tok),
        out_shape=jax.ShapeDtypeStruct((t_tokens, D_MODEL), F32),
        compiler_params=pltpu.CompilerParams(
            dimension_semantics=("arbitrary",), vmem_limit_bytes=VMEM_LIMIT_BYTES),
        name="mix_ffn_pool" if pool else "mix_ffn_ret",
    )(*args)


def _pool_ffn_kernel(*refs, final, tiles_per_seq):
    refs = list(refs)
    (xc_ref, xn_ref, xh_ref, mod_ref, modn_ref, inv0_ref, invn_ref, n1w_ref, n2w_ref, wmix_ref, ps_ref,
     win_ref, wout_ref) = refs[:13]
    rest = refs[13:]
    fnw_ref = rest.pop(0) if final else None
    o_ref, d_scr, h_scr = rest
    tm = xc_ref.shape[0]
    hal = POOL_HALO
    i = pl.program_id(0)

    @pl.when(i == 0)
    def _():
        gain = n1w_ref[...] * (1.0 + mod_ref[1:2, :])
        _pool_tile(xc_ref, xc_ref[0:hal, :], xn_ref[0:hal, :], False, tiles_per_seq > 1,
                   gain, mod_ref[0:1, :], inv0_ref, h_scr, d_scr)

    def pool_next():
        j = (i + 1) % tiles_per_seq
        gain = n1w_ref[...] * (1.0 + modn_ref[1:2, :])
        return _pool_tile(xn_ref, xc_ref[tm - hal:tm, :], xh_ref[...], j > 0, j < tiles_per_seq - 1,
                   gain, modn_ref[0:1, :], invn_ref, h_scr, d_scr)

    _ffn_pipeline(lambda rows: _pool_mix(d_scr[rows, :], wmix_ref, ps_ref),
                  xc_ref, mod_ref, n2w_ref, win_ref, wout_ref, fnw_ref, o_ref, between=pool_next)


def _pool_ffn(x, mod, n1w, n2w, pool_params, ffn_w_in, ffn_w_out, layer, final_w, seq_len):
    t_tokens = x.shape[0]
    tm = TOKEN_TILE
    n_steps = t_tokens // tm
    tiles_per_seq = seq_len // tm
    halo_per_tile = tm // POOL_HALO
    pool_w, pool_idx, pool_scale = pool_params
    final = final_w is not None
    n_groups = len(POOL_WINDOWS)
    nxt = lambda i: jnp.minimum(i + 1, n_steps - 1)
    in_specs = [
        pl.BlockSpec((tm, D_MODEL), lambda i: (i, 0)),
        pl.BlockSpec((tm, D_MODEL), lambda i: (nxt(i), 0)),
        pl.BlockSpec((POOL_HALO, D_MODEL), lambda i: (jnp.minimum(i + 2, n_steps - 1) * halo_per_tile, 0)),
        pl.BlockSpec((None, N_MOD, D_MODEL), lambda i: (i // tiles_per_seq, 0, 0)),
        pl.BlockSpec((None, N_MOD, D_MODEL), lambda i: (nxt(i) // tiles_per_seq, 0, 0)),
        _resident((n_groups, tm, LANES)),
        pl.BlockSpec((n_groups, tm, LANES), lambda i: (0, nxt(i) % tiles_per_seq, 0)),
        _resident((1, D_MODEL)),
        _resident((1, D_MODEL)),
        _layer_resident(pool_w, pool_idx),
        _resident((1, D_MODEL)),
        _layer_resident(ffn_w_in, layer),
        _layer_resident(ffn_w_out, layer),
    ]
    inv = _pool_inv_counts(seq_len)
    args = [x, x, x, mod, mod, inv, inv, n1w.reshape(1, D_MODEL), n2w.reshape(1, D_MODEL), pool_w,
            pool_scale.reshape(1, D_MODEL), ffn_w_in, ffn_w_out]
    if final:
        in_specs.append(_resident((1, D_MODEL)))
        args.append(final_w.reshape(1, D_MODEL))
    return pl.pallas_call(
        functools.partial(_pool_ffn_kernel, final=final, tiles_per_seq=tiles_per_seq),
        grid=(n_steps,),
        in_specs=in_specs,
        out_specs=pl.BlockSpec((tm, D_MODEL), lambda i: (i, 0)),
        out_shape=jax.ShapeDtypeStruct((t_tokens, D_MODEL), F32),
        scratch_shapes=[pltpu.VMEM((tm, D_MODEL), BF16),
                        pltpu.VMEM((tm + 2 * POOL_HALO, D_MODEL), F32)],
        compiler_params=pltpu.CompilerParams(
            dimension_semantics=("arbitrary",), vmem_limit_bytes=VMEM_LIMIT_BYTES),
        name="pool_ffn",
    )(*args)


def _rope_tables(seq_len):
    rows = seq_len // GRID_W
    row = jnp.repeat(jnp.arange(rows), GRID_W).astype(F32)
    col = jnp.tile(jnp.arange(GRID_W), rows).astype(F32)
    d = RET_QK_DIM // 2
    inv = ROPE_THETA ** (-jnp.arange(0, d, 2, dtype=F32) / d)
    ang_r = row[:, None] * inv[None, :]
    ang_c = col[:, None] * inv[None, :]
    cos = jnp.concatenate([jnp.cos(ang_r), jnp.cos(ang_r), jnp.cos(ang_c), jnp.cos(ang_c)], axis=-1)
    sin = jnp.concatenate([-jnp.sin(ang_r), jnp.sin(ang_r), -jnp.sin(ang_c), jnp.sin(ang_c)], axis=-1)
    return cos, sin


def kernel(x, c, ctx, c_ctx, ada_w, ada_b, norm1_w, norm2_w, ret_w_in, ret_log_decay, ret_gn_w,
           ret_w_out, pool_w, pool_scale, ffn_w_in, ffn_w_out, final_norm_w):
    bsz, seq_len, _ = x.shape
    ctx_len = ctx.shape[1]
    depth = ada_w.shape[0]

    pad = (-(bsz + 1)) % 8
    c_all = jnp.concatenate([c, c_ctx[None, :], jnp.zeros((pad, D_MODEL), F32)], axis=0)
    mods = _mods(c_all, ada_w, ada_b).reshape(depth, bsz + 1 + pad, N_MOD, D_MODEL)

    rope_tabs = _rope_tables(seq_len)
    ret_w_in_b = ret_w_in.astype(BF16)
    ret_w_out_b = ret_w_out.astype(BF16)
    pool_w_b = pool_w.astype(BF16)
    ffn_w_in_b = ffn_w_in.astype(BF16)
    ffn_w_out_b = ffn_w_out.astype(BF16)

    x_lat = x.reshape(bsz * seq_len, D_MODEL)
    x_ctx = ctx.reshape(bsz * ctx_len, D_MODEL)
    for i in range(depth):
        last = i == depth - 1
        use_ret = (i % N_MIXERS) == 0
        j = i // N_MIXERS
        if last and use_ret:
            raise NotImplementedError("a final retention layer (context-state-only path) is not built")
        mod_lat = mods[i, :bsz]
        mod_ctx = mods[i, bsz:bsz + 1]
        final_w = final_norm_w if last else None
        if use_ret:
            ql, kl, vl, gl = _ret_proj(x_lat, mod_lat, norm1_w[i], ret_w_in_b, j, rope_tabs, seq_len)
            qc, kc, vc, gc = _ret_proj(x_ctx, mod_ctx, norm1_w[i], ret_w_in_b, j, None, ctx_len)
            r3 = lambda a, n: a.reshape(bsz, n, a.shape[-1])
            o_ctx, o_lat = _retention(
                ret_log_decay[j], ret_gn_w[j], ret_w_out_b, j,
                r3(qc, ctx_len), r3(kc, ctx_len), r3(vc, ctx_len), r3(gc, ctx_len),
                r3(ql, seq_len), r3(kl, seq_len), r3(vl, seq_len), r3(gl, seq_len))
            x_lat = _mix_ffn(o_lat.reshape(bsz * seq_len, D_MODEL), x_lat, mod_lat, norm2_w[i], None,
                             ffn_w_in_b, ffn_w_out_b, i, final_w, seq_len)
            x_ctx = _mix_ffn(o_ctx.reshape(bsz * ctx_len, D_MODEL), x_ctx, mod_ctx, norm2_w[i], None,
                             ffn_w_in_b, ffn_w_out_b, i, None, ctx_len)
        else:
            pool_params = (pool_w_b, j, pool_scale[j])
            if not last:
                d_ctx = _pool(x_ctx, mod_ctx, norm1_w[i], ctx_len)
                x_ctx = _mix_ffn(d_ctx, x_ctx, mod_ctx, norm2_w[i], pool_params,
                                 ffn_w_in_b, ffn_w_out_b, i, None, ctx_len)
            x_lat = _pool_ffn(x_lat, mod_lat, norm1_w[i], norm2_w[i], pool_params,
                              ffn_w_in_b, ffn_w_out_b, i, final_w, seq_len)
    return x_lat.reshape(bsz, seq_len, D_MODEL)
```

```python
import functools

import jax
import jax.numpy as jnp
from jax import lax
from jax.experimental import pallas as pl
from jax.experimental.pallas import tpu as pltpu

D_MODEL = 1024
GRID_W = 64
N_MIXERS = 2
RET_HEADS = 4
RET_QK_DIM = D_MODEL // RET_HEADS
RET_V_DIM = 2 * D_MODEL // RET_HEADS
RET_QK_TOTAL = RET_HEADS * RET_QK_DIM
RET_V_TOTAL = RET_HEADS * RET_V_DIM
RET_IN_TOTAL = 2 * RET_QK_TOTAL + 2 * RET_V_TOTAL
POOL_WINDOWS = (2, 4, 8, 16)
POOL_GROUP = D_MODEL // len(POOL_WINDOWS)
POOL_HALO = 8
ROPE_THETA = 10000.0
EPS = 1e-6
GN_EPS = 1e-5
N_MOD = 6

F32 = jnp.float32
BF16 = jnp.bfloat16

VMEM_LIMIT_BYTES = 56 * 1024 * 1024
TOKEN_TILE = 512
SUB_TILE = 256
FFN_HIDDEN_CHUNK = 256
POOL_TILE = 256
POOL_NORM_ROWS = 8
POOL_SUM_ROWS = 128
LANES = 128
RET_CHUNK = 256
MODS_COL_TILE = 1536


def _dot(a, b):
    return jnp.dot(a, b, preferred_element_type=F32)


def _rmsnorm(x, w):
    return x * lax.rsqrt(jnp.mean(x * x, axis=-1, keepdims=True) + EPS) * w


def _norm_mod(x, w, scale, shift):
    return _rmsnorm(x, w * (1.0 + scale)) + shift


def _resident(shape):
    nd = len(shape)
    return pl.BlockSpec(shape, lambda *_: (0,) * nd, pipeline_mode=pl.Buffered(1))


def _layer_resident(stacked, layer):
    nd = stacked.ndim - 1
    return pl.BlockSpec((None,) + stacked.shape[1:], lambda *_: (layer,) + (0,) * nd,
                        pipeline_mode=pl.Buffered(1))


def _mods_kernel(c_ref, w_ref, b_ref, o_ref):
    s = jax.nn.silu(c_ref[...]).astype(BF16)
    o_ref[...] = _dot(s, w_ref[...].astype(BF16)) + b_ref[...]


def _mods(c_all, ada_w, ada_b):
    depth, _, n = ada_w.shape
    r = c_all.shape[0]
    tn = MODS_COL_TILE
    return pl.pallas_call(
        _mods_kernel,
        grid=(depth, n // tn),
        in_specs=[
            pl.BlockSpec((r, D_MODEL), lambda l, j: (0, 0)),
            pl.BlockSpec((None, D_MODEL, tn), lambda l, j: (l, 0, j)),
            pl.BlockSpec((None, 1, tn), lambda l, j: (l, 0, j)),
        ],
        out_specs=pl.BlockSpec((None, r, tn), lambda l, j: (l, 0, j)),
        out_shape=jax.ShapeDtypeStruct((depth, r, n), F32),
        compiler_params=pltpu.CompilerParams(
            dimension_semantics=("arbitrary", "arbitrary"), vmem_limit_bytes=VMEM_LIMIT_BYTES),
        name="adaln_mods",
    )(c_all, ada_w, ada_b.reshape(depth, 1, n))


def _rope_half(t, cos, sin):
    return t * cos + pltpu.roll(t, 64, axis=1) * sin


def _ret_proj_kernel(*refs, rope):
    if rope:
        x_ref, mod_ref, nw_ref, w_ref, cos_ref, sin_ref, q_ref, k_ref, v_ref, g_ref = refs
    else:
        x_ref, mod_ref, nw_ref, w_ref, q_ref, k_ref, v_ref, g_ref = refs
    q_scale = RET_QK_DIM ** -0.5
    half = RET_QK_DIM // 2
    v_lo = 2 * RET_QK_TOTAL

    def norm1(rows):
        return _norm_mod(x_ref[rows, :], nw_ref[...], mod_ref[1:2, :], mod_ref[0:1, :]).astype(BF16)

    def qk_proj(rows, hb, out_ref, base, scale):
        for hd in range(RET_HEADS):
            lo = hd * RET_QK_DIM
            t = _dot(hb, w_ref[:, base + lo:base + lo + RET_QK_DIM])
            if rope:
                t = jnp.concatenate(
                    [_rope_half(t[:, :half], cos_ref[rows, :half], sin_ref[rows, :half]),
                     _rope_half(t[:, half:], cos_ref[rows, half:], sin_ref[rows, half:])], axis=-1)
            if scale is not None:
                t = t * scale
            out_ref[rows, lo:lo + RET_QK_DIM] = t.astype(BF16)

    def vg_proj(rows, hb):
        v_ref[rows, :] = _dot(hb, w_ref[:, v_lo:v_lo + RET_V_TOTAL]).astype(BF16)
        g_ref[rows, :] = _dot(hb, w_ref[:, v_lo + RET_V_TOTAL:]).astype(BF16)

    n_sub = x_ref.shape[0] // SUB_TILE
    rows = [pl.ds(s * SUB_TILE, SUB_TILE) for s in range(n_sub)]
    hb = norm1(rows[0])
    for s in range(n_sub):
        qk_proj(rows[s], hb, q_ref, 0, q_scale)
        hb_next = norm1(rows[s + 1]) if s + 1 < n_sub else None
        qk_proj(rows[s], hb, k_ref, RET_QK_TOTAL, None)
        vg_proj(rows[s], hb)
        hb = hb_next


def _ret_proj(x, mod, nw, w_in, layer, rope_tabs, seq_len):
    t_tokens = x.shape[0]
    tm = TOKEN_TILE
    tiles_per_seq = seq_len // tm
    per_batch = mod.shape[0] > 1
    mod_map = (lambda i: (i // tiles_per_seq, 0, 0)) if per_batch else (lambda i: (0, 0, 0))
    tok = lambda i: (i, 0)
    in_specs = [
        pl.BlockSpec((tm, D_MODEL), tok),
        pl.BlockSpec((None, N_MOD, D_MODEL), mod_map),
        _resident((1, D_MODEL)),
        _layer_resident(w_in, layer),
    ]
    args = [x, mod, nw.reshape(1, D_MODEL), w_in]
    rope = rope_tabs is not None
    if rope:
        pos = lambda i: (i % tiles_per_seq, 0)
        in_specs += [pl.BlockSpec((tm, RET_QK_DIM), pos), pl.BlockSpec((tm, RET_QK_DIM), pos)]
        args += list(rope_tabs)
    return pl.pallas_call(
        functools.partial(_ret_proj_kernel, rope=rope),
        grid=(t_tokens // tm,),
        in_specs=in_specs,
        out_specs=[
            pl.BlockSpec((tm, RET_QK_TOTAL), tok),
            pl.BlockSpec((tm, RET_QK_TOTAL), tok),
            pl.BlockSpec((tm, RET_V_TOTAL), tok),
            pl.BlockSpec((tm, RET_V_TOTAL), tok),
        ],
        out_shape=[
            jax.ShapeDtypeStruct((t_tokens, RET_QK_TOTAL), BF16),
            jax.ShapeDtypeStruct((t_tokens, RET_QK_TOTAL), BF16),
            jax.ShapeDtypeStruct((t_tokens, RET_V_TOTAL), BF16),
            jax.ShapeDtypeStruct((t_tokens, RET_V_TOTAL), BF16),
        ],
        compiler_params=pltpu.CompilerParams(
            dimension_semantics=("arbitrary",), vmem_limit_bytes=VMEM_LIMIT_BYTES),
        name="ret_proj",
    )(*args)


def _retention_kernel(lg_ref, gnw_ref, wout_ref, qc_ref, kc_ref, vc_ref, gc_ref, ql_ref, kl_ref, vl_ref,
                      gl_ref, oc_ref, ol_ref, y_scr, intra_scr, *, n_ctx, n_lat):
    c = RET_CHUNK
    hd = pl.program_id(1)
    lg_f = -jnp.abs(lg_ref[0, hd])
    lg_b = -jnp.abs(lg_ref[1, hd])

    @pl.when(pl.program_id(0) == 0)
    def _():
        ii = lax.broadcasted_iota(jnp.int32, (c, c), 0)
        jj = lax.broadcasted_iota(jnp.int32, (c, c), 1)
        diff = (ii - jj).astype(F32)
        intra_scr[hd] = jnp.exp(jnp.where(diff >= 0, lg_f * diff, lg_b * (-diff)))

    intra = intra_scr[hd]
    idx = lax.broadcasted_iota(jnp.int32, (c, 1), 0).astype(F32)
    idx_row = lax.broadcasted_iota(jnp.int32, (1, c), 1).astype(F32)
    qd_f = jnp.exp(lg_f * (idx + 1.0))
    kd_f = jnp.exp(lg_f * (c - 1.0 - idx_row))
    qd_b = jnp.exp(lg_b * (c - idx))
    kd_b = jnp.exp(lg_b * idx_row)
    cd_f = jnp.exp(jnp.full((1, 1), lg_f * c, F32))
    cd_b = jnp.exp(jnp.full((1, 1), lg_b * c, F32))
    gnw = gnw_ref[...]

    @pl.when(hd == 0)
    def _():
        oc_ref[...] = jnp.zeros_like(oc_ref)
        ol_ref[...] = jnp.zeros_like(ol_ref)

    def chunk_refs(kind, j):
        rows = pl.ds(j * c, c)
        if kind == "c":
            return qc_ref.at[rows], kc_ref.at[rows], vc_ref.at[rows], gc_ref.at[rows], oc_ref.at[rows]
        return ql_ref.at[rows], kl_ref.at[rows], vl_ref.at[rows], gl_ref.at[rows], ol_ref.at[rows]

    def inter(q_r, k_r, v_r, state, qd, kd, cd):
        kdt = (k_r[...].T.astype(F32) * kd).astype(BF16)
        upd = _dot(kdt, v_r[...])
        if state is None:
            return None, upd
        out = _dot((q_r[...].astype(F32) * qd).astype(BF16), state.astype(BF16))
        return out, state * cd + upd

    def intra_chunk(q_r, k_r, v_r):
        scores = lax.dot_general(q_r[...], k_r[...], (((1,), (1,)), ((), ())), preferred_element_type=F32)
        return _dot((scores * intra).astype(BF16), v_r[...])

    def finalize(y, g_r, o_r):
        mu = jnp.mean(y, axis=-1, keepdims=True)
        yc = y - mu
        var = jnp.mean(yc * yc, axis=-1, keepdims=True)
        yn = yc * lax.rsqrt(var + GN_EPS) * gnw
        z = (jax.nn.silu(g_r[...].astype(F32)) * yn).astype(BF16)
        o_r[...] += _dot(z, wout_ref[...])

    fwd_order = [("c", j) for j in range(n_ctx)] + [("l", j) for j in range(n_lat)]
    bwd_order = [("c", j) for j in reversed(range(n_ctx))] + [("l", j) for j in reversed(range(n_lat))]
    offset = {"c": 0, "l": n_ctx * c}
    first_visit = {}
    complete = []
    s_f = s_b = None
    for key_f, key_b in zip(fwd_order, bwd_order):
        q_r, k_r, v_r, g_f, o_f = chunk_refs(*key_f)
        out_i = intra_chunk(q_r, k_r, v_r)
        qb_r, kb_r, vb_r, g_b, o_b = chunk_refs(*key_b)
        out_b, s_b = inter(qb_r, kb_r, vb_r, s_b, qd_b, kd_b, cd_b)
        out_f, s_f = inter(q_r, k_r, v_r, s_f, qd_f, kd_f, cd_f)
        out_f = out_i if out_f is None else out_i + out_f
        if key_f == key_b:
            complete.append((out_f if out_b is None else out_f + out_b, g_f, o_f))
            continue
        for key, out, g_r, o_r in ((key_f, out_f, g_f, o_f), (key_b, out_b, g_b, o_b)):
            rows = pl.ds(offset[key[0]] + key[1] * c, c)
            if key not in first_visit:
                if out is not None:
                    y_scr[rows, :] = out
                first_visit[key] = out is not None
            elif not first_visit[key]:
                complete.append((out, g_r, o_r))
            else:
                complete.append((y_scr[rows, :] if out is None else y_scr[rows, :] + out, g_r, o_r))
    assert len(complete) == n_ctx + n_lat
    for y, g_r, o_r in complete:
        finalize(y, g_r, o_r)


def _retention(log_decay, gn_w, w_out, layer, qc, kc, vc, gc, ql, kl, vl, gl):
    bsz, ctx_len, _ = qc.shape
    seq_len = ql.shape[1]
    n_ctx, n_lat = ctx_len // RET_CHUNK, seq_len // RET_CHUNK
    bh = lambda b, h: (b, 0, h)
    qk_c = pl.BlockSpec((None, ctx_len, RET_QK_DIM), bh)
    v_c = pl.BlockSpec((None, ctx_len, RET_V_DIM), bh)
    qk_l = pl.BlockSpec((None, seq_len, RET_QK_DIM), bh)
    v_l = pl.BlockSpec((None, seq_len, RET_V_DIM), bh)
    return pl.pallas_call(
        functools.partial(_retention_kernel, n_ctx=n_ctx, n_lat=n_lat),
        grid=(bsz, RET_HEADS),
        in_specs=[
            pl.BlockSpec(memory_space=pltpu.SMEM),
            pl.BlockSpec((1, RET_V_DIM), lambda b, h: (0, h)),
            pl.BlockSpec((None, RET_V_DIM, D_MODEL), lambda b, h: (layer, h, 0)),
            qk_c, qk_c, v_c, v_c, qk_l, qk_l, v_l, v_l,
        ],
        out_specs=[
            pl.BlockSpec((None, ctx_len, D_MODEL), lambda b, h: (b, 0, 0)),
            pl.BlockSpec((None, seq_len, D_MODEL), lambda b, h: (b, 0, 0)),
        ],
        out_shape=[
            jax.ShapeDtypeStruct((bsz, ctx_len, D_MODEL), F32),
            jax.ShapeDtypeStruct((bsz, seq_len, D_MODEL), F32),
        ],
        scratch_shapes=[pltpu.VMEM((ctx_len + seq_len, RET_V_DIM), F32),
                        pltpu.VMEM((RET_HEADS, RET_CHUNK, RET_CHUNK), F32)],
        compiler_params=pltpu.CompilerParams(
            dimension_semantics=("arbitrary", "arbitrary"), vmem_limit_bytes=VMEM_LIMIT_BYTES),
        name="retention",
    )(log_decay, gn_w.reshape(1, RET_V_TOTAL), w_out, qc, kc, vc, gc, ql, kl, vl, gl)


def _pool_tile(x_ref, x_prev, x_next, has_prev, has_next, gain, shift, inv_ref, h_scr, d_ref):
    tm = x_ref.shape[0]
    hal = POOL_HALO
    h_scr[0:hal, :] = jnp.where(has_prev, _rmsnorm(x_prev, gain) + shift, 0.0)
    for r in range(0, tm, POOL_NORM_ROWS):
        h_scr[hal + r:hal + r + POOL_NORM_ROWS, :] = _rmsnorm(x_ref[r:r + POOL_NORM_ROWS, :], gain) + shift
    h_scr[hal + tm:, :] = jnp.where(has_next, _rmsnorm(x_next, gain) + shift, 0.0)

    pb = min(POOL_SUM_ROWS, tm)
    n_ext = pb + 2 * hal

    def shifted(a, k):
        return pltpu.roll(a, k % n_ext, axis=0)

    token = jnp.zeros((8, LANES), jnp.int32)
    for r in range(0, tm, pb):
        for gi, w in enumerate(POOL_WINDOWS):
            assert w // 2 <= hal
            inv_cnt = inv_ref[gi, r:r + pb, :]
            for lo in range(gi * POOL_GROUP, (gi + 1) * POOL_GROUP, LANES):
                e = h_scr[r:r + n_ext, lo:lo + LANES]
                c = e
                a = 1
                while a < w:
                    c = c + shifted(c, a)
                    a *= 2
                s = shifted(c, -(w // 2 - 1)) if w > 2 else c
                d = s[hal:hal + pb, :] * inv_cnt - e[hal:hal + pb, :]
                d_ref[r:r + pb, lo:lo + LANES] = d.astype(BF16)
                bits = lax.bitcast_convert_type(d, jnp.int32)
                for q in range(0, pb, 8):
                    token = token | bits[q:q + 8, :]
    return token


def _pool_kernel(x_ref, xp_ref, xn_ref, mod_ref, nw_ref, inv_ref, d_ref, h_scr):
    i = pl.program_id(1)
    gain = nw_ref[...] * (1.0 + mod_ref[1:2, :])
    _pool_tile(x_ref, xp_ref[...], xn_ref[...], i > 0, i < pl.num_programs(1) - 1,
               gain, mod_ref[0:1, :], inv_ref, h_scr, d_ref)


def _pool_inv_counts(seq_len):
    t = jnp.arange(seq_len)
    inv = [1.0 / (jnp.minimum(t + w // 2, seq_len) - jnp.maximum(t - w // 2, 0)).astype(F32)
           for w in POOL_WINDOWS]
    return jnp.broadcast_to(jnp.stack(inv)[:, :, None], (len(POOL_WINDOWS), seq_len, LANES))


def _pool(x, mod, nw, seq_len):
    t_tokens = x.shape[0]
    bsz = t_tokens // seq_len
    tm = min(POOL_TILE, seq_len)
    n_tiles = seq_len // tm
    halo_per_tile = tm // POOL_HALO
    n_halo = seq_len // POOL_HALO
    per_batch = mod.shape[0] > 1
    x3 = x.reshape(bsz, seq_len, D_MODEL)
    n_groups = len(POOL_WINDOWS)
    out = pl.pallas_call(
        _pool_kernel,
        grid=(bsz, n_tiles),
        in_specs=[
            pl.BlockSpec((None, tm, D_MODEL), lambda b, i: (b, i, 0)),
            pl.BlockSpec((None, POOL_HALO, D_MODEL),
                         lambda b, i: (b, jnp.maximum(i * halo_per_tile - 1, 0), 0)),
            pl.BlockSpec((None, POOL_HALO, D_MODEL),
                         lambda b, i: (b, jnp.minimum((i + 1) * halo_per_tile, n_halo - 1), 0)),
            pl.BlockSpec((None, N_MOD, D_MODEL),
                         (lambda b, i: (b, 0, 0)) if per_batch else (lambda b, i: (0, 0, 0))),
            _resident((1, D_MODEL)),
            pl.BlockSpec((n_groups, tm, LANES), lambda b, i: (0, i, 0)),
        ],
        out_specs=pl.BlockSpec((None, tm, D_MODEL), lambda b, i: (b, i, 0)),
        out_shape=jax.ShapeDtypeStruct((bsz, seq_len, D_MODEL), BF16),
        scratch_shapes=[pltpu.VMEM((tm + 2 * POOL_HALO, D_MODEL), F32)],
        compiler_params=pltpu.CompilerParams(
            dimension_semantics=("arbitrary", "arbitrary"), vmem_limit_bytes=VMEM_LIMIT_BYTES),
        name="pool_mix",
    )(x3, x3, x3, mod, nw.reshape(1, D_MODEL), _pool_inv_counts(seq_len))
    return out.reshape(t_tokens, D_MODEL)


def _pool_mix(d, wmix_ref, ps_ref):
    return jnp.concatenate(
        [_dot(d[:, g * POOL_GROUP:(g + 1) * POOL_GROUP], wmix_ref[g]) for g in range(len(POOL_WINDOWS))],
        axis=-1) * ps_ref[...]


def _ffn_pipeline(mix, x_ref, mod_ref, n2w_ref, win_ref, wout_ref, fnw_ref, o_ref, between=None):
    hidden = win_ref.shape[-1] // 2

    def norm2(rows, y):
        x1 = x_ref[rows, :] + mod_ref[2:3, :] * y
        h2 = _norm_mod(x1, n2w_ref[...], mod_ref[4:5, :], mod_ref[3:4, :]).astype(BF16)
        return x1, h2

    def gate(ab):
        return (jax.nn.silu(ab[:, :hidden]) * ab[:, hidden:]).astype(BF16)

    def finish(rows, x1, f, gate2=None):
        x2 = x1 + (mod_ref[5:6, :] if gate2 is None else gate2) * f
        if fnw_ref is not None:
            x2 = _rmsnorm(x2, fnw_ref[...])
        o_ref[rows, :] = x2

    assert x_ref.shape[0] == 2 * SUB_TILE
    ra, rb = (pl.ds(s * SUB_TILE, SUB_TILE) for s in range(2))
    y_a = mix(ra)
    y_b = mix(rb)
    x1_a, h2_a = norm2(ra, y_a)
    x1_b, h2_b = norm2(rb, y_b)

    def ffn_chunk(h2, j):
        lo = j * FFN_HIDDEN_CHUNK
        a = _dot(h2, win_ref[:, lo:lo + FFN_HIDDEN_CHUNK])
        b = _dot(h2, win_ref[:, hidden + lo:hidden + lo + FFN_HIDDEN_CHUNK])
        u = (jax.nn.silu(a) * b).astype(BF16)
        return _dot(u, wout_ref[lo:lo + FFN_HIDDEN_CHUNK, :])

    assert hidden % FFN_HIDDEN_CHUNK == 0
    f_a = f_b = None
    for j in range(hidden // FFN_HIDDEN_CHUNK):
        d_a, d_b = ffn_chunk(h2_a, j), ffn_chunk(h2_b, j)
        f_a = d_a if f_a is None else f_a + d_a
        f_b = d_b if f_b is None else f_b + d_b
    finish(ra, x1_a, f_a)
    gate2 = None
    if between is not None:
        token = between()
        zero = lax.shift_right_logical(lax.shift_right_logical(token, 16), 16)[0:1, :].astype(F32)
        gate2 = mod_ref[5:6, :] + jnp.concatenate([zero] * (D_MODEL // LANES), axis=1)
    finish(rb, x1_b, f_b, gate2)


def _mix_ffn_kernel(z_ref, x_ref, mod_ref, n2w_ref, wmix_ref, ps_ref, win_ref, wout_ref, o_ref):
    _ffn_pipeline(lambda rows: _pool_mix(z_ref[rows, :], wmix_ref, ps_ref),
                  x_ref, mod_ref, n2w_ref, win_ref, wout_ref, None, o_ref)


def _mix_ffn(z, x, mod, n2w, pool_params, ffn_w_in, ffn_w_out, layer, seq_len):
    t_tokens = x.shape[0]
    tm = TOKEN_TILE
    tiles_per_seq = seq_len // tm
    per_batch = mod.shape[0] > 1
    mod_map = (lambda i: (i // tiles_per_seq, 0, 0)) if per_batch else (lambda i: (0, 0, 0))
    tok = lambda i: (i, 0)
    pool_w, pool_idx, pool_scale = pool_params
    return pl.pallas_call(
        _mix_ffn_kernel,
        grid=(t_tokens // tm,),
        in_specs=[
            pl.BlockSpec((tm, D_MODEL), tok),
            pl.BlockSpec((tm, D_MODEL), tok),
            pl.BlockSpec((None, N_MOD, D_MODEL), mod_map),
            _resident((1, D_MODEL)),
            _layer_resident(pool_w, pool_idx),
            _resident((1, D_MODEL)),
            _layer_resident(ffn_w_in, layer),
            _layer_resident(ffn_w_out, layer),
        ],
        out_specs=pl.BlockSpec((tm, D_MODEL), tok),
        out_shape=jax.ShapeDtypeStruct((t_tokens, D_MODEL), F32),
        compiler_params=pltpu.CompilerParams(
            dimension_semantics=("arbitrary",), vmem_limit_bytes=VMEM_LIMIT_BYTES),
        name="mix_ffn_pool",
    )(z, x, mod, n2w.reshape(1, D_MODEL), pool_w, pool_scale.reshape(1, D_MODEL), ffn_w_in, ffn_w_out)


def _ret_ffn_kernel(*refs, final):
    refs = list(refs)
    z_ref, x_ref, mod_ref, n2w_ref, win_ref, wout_ref = refs[:6]
    fnw_ref = refs[6] if final else None
    o_ref = refs[-1]
    _ffn_pipeline(lambda rows: z_ref[rows, :], x_ref, mod_ref, n2w_ref, win_ref, wout_ref, fnw_ref, o_ref)


def _ret_ffn(z, x, mod, n2w, ffn_w_in, ffn_w_out, layer, final_w, seq_len):
    t_tokens = x.shape[0]
    tm = TOKEN_TILE
    tiles_per_seq = seq_len // tm
    per_batch = mod.shape[0] > 1
    mod_map = (lambda i: (i // tiles_per_seq, 0, 0)) if per_batch else (lambda i: (0, 0, 0))
    tok = lambda i: (i, 0)
    final = final_w is not None
    in_specs = [
        pl.BlockSpec((tm, D_MODEL), tok),
        pl.BlockSpec((tm, D_MODEL), tok),
        pl.BlockSpec((None, N_MOD, D_MODEL), mod_map),
        _resident((1, D_MODEL)),
        _layer_resident(ffn_w_in, layer),
        _layer_resident(ffn_w_out, layer),
    ]
    args = [z, x, mod, n2w.reshape(1, D_MODEL), ffn_w_in, ffn_w_out]
    if final:
        in_specs.append(_resident((1, D_MODEL)))
        args.append(final_w.reshape(1, D_MODEL))
    return pl.pallas_call(
        functools.partial(_ret_ffn_kernel, final=final),
        grid=(t_tokens // tm,),
        in_specs=in_specs,
        out_specs=pl.BlockSpec((tm, D_MODEL), tok),
        out_shape=jax.ShapeDtypeStruct((t_tokens, D_MODEL), F32),
        compiler_params=pltpu.CompilerParams(
            dimension_semantics=("arbitrary",), vmem_limit_bytes=VMEM_LIMIT_BYTES),
        name="ret_ffn",
    )(*args)


def _pool_ffn_kernel(*refs, final, tiles_per_seq):
    refs = list(refs)
    (xc_ref, xn_ref, xh_ref, mod_ref, modn_ref, inv0_ref, invn_ref, n1w_ref, n2w_ref, wmix_ref, ps_ref,
     win_ref, wout_ref) = refs[:13]
    rest = refs[13:]
    fnw_ref = rest.pop(0) if final else None
    o_ref, d_scr, h_scr = rest
    tm = xc_ref.shape[0]
    hal = POOL_HALO
    i = pl.program_id(0)

    @pl.when(i == 0)
    def _():
        gain = n1w_ref[...] * (1.0 + mod_ref[1:2, :])
        _pool_tile(xc_ref, xc_ref[0:hal, :], xn_ref[0:hal, :], False, tiles_per_seq > 1,
                   gain, mod_ref[0:1, :], inv0_ref, h_scr, d_scr)

    def pool_next():
        j = (i + 1) % tiles_per_seq
        gain = n1w_ref[...] * (1.0 + modn_ref[1:2, :])
        return _pool_tile(xn_ref, xc_ref[tm - hal:tm, :], xh_ref[...], j > 0, j < tiles_per_seq - 1,
                   gain, modn_ref[0:1, :], invn_ref, h_scr, d_scr)

    _ffn_pipeline(lambda rows: _pool_mix(d_scr[rows, :], wmix_ref, ps_ref),
                  xc_ref, mod_ref, n2w_ref, win_ref, wout_ref, fnw_ref, o_ref, between=pool_next)


def _pool_ffn(x, mod, n1w, n2w, pool_params, ffn_w_in, ffn_w_out, layer, final_w, seq_len):
    t_tokens = x.shape[0]
    tm = TOKEN_TILE
    n_steps = t_tokens // tm
    tiles_per_seq = seq_len // tm
    halo_per_tile = tm // POOL_HALO
    pool_w, pool_idx, pool_scale = pool_params
    final = final_w is not None
    n_groups = len(POOL_WINDOWS)
    nxt = lambda i: jnp.minimum(i + 1, n_steps - 1)
    in_specs = [
        pl.BlockSpec((tm, D_MODEL), lambda i: (i, 0)),
        pl.BlockSpec((tm, D_MODEL), lambda i: (nxt(i), 0)),
        pl.BlockSpec((POOL_HALO, D_MODEL), lambda i: (jnp.minimum(i + 2, n_steps - 1) * halo_per_tile, 0)),
        pl.BlockSpec((None, N_MOD, D_MODEL), lambda i: (i // tiles_per_seq, 0, 0)),
        pl.BlockSpec((None, N_MOD, D_MODEL), lambda i: (nxt(i) // tiles_per_seq, 0, 0)),
        _resident((n_groups, tm, LANES)),
        pl.BlockSpec((n_groups, tm, LANES), lambda i: (0, nxt(i) % tiles_per_seq, 0)),
        _resident((1, D_MODEL)),
        _resident((1, D_MODEL)),
        _layer_resident(pool_w, pool_idx),
        _resident((1, D_MODEL)),
        _layer_resident(ffn_w_in, layer),
        _layer_resident(ffn_w_out, layer),
    ]
    inv = _pool_inv_counts(seq_len)
    args = [x, x, x, mod, mod, inv, inv, n1w.reshape(1, D_MODEL), n2w.reshape(1, D_MODEL), pool_w,
            pool_scale.reshape(1, D_MODEL), ffn_w_in, ffn_w_out]
    if final:
        in_specs.append(_resident((1, D_MODEL)))
        args.append(final_w.reshape(1, D_MODEL))
    return pl.pallas_call(
        functools.partial(_pool_ffn_kernel, final=final, tiles_per_seq=tiles_per_seq),
        grid=(n_steps,),
        in_specs=in_specs,
        out_specs=pl.BlockSpec((tm, D_MODEL), lambda i: (i, 0)),
        out_shape=jax.ShapeDtypeStruct((t_tokens, D_MODEL), F32),
        scratch_shapes=[pltpu.VMEM((tm, D_MODEL), BF16),
                        pltpu.VMEM((tm + 2 * POOL_HALO, D_MODEL), F32)],
        compiler_params=pltpu.CompilerParams(
            dimension_semantics=("arbitrary",), vmem_limit_bytes=VMEM_LIMIT_BYTES),
        name="pool_ffn",
    )(*args)


def _rope_tables(seq_len):
    rows = seq_len // GRID_W
    row = jnp.repeat(jnp.arange(rows), GRID_W).astype(F32)
    col = jnp.tile(jnp.arange(GRID_W), rows).astype(F32)
    d = RET_QK_DIM // 2
    inv = ROPE_THETA ** (-jnp.arange(0, d, 2, dtype=F32) / d)
    ang_r = row[:, None] * inv[None, :]
    ang_c = col[:, None] * inv[None, :]
    cos = jnp.concatenate([jnp.cos(ang_r), jnp.cos(ang_r), jnp.cos(ang_c), jnp.cos(ang_c)], axis=-1)
    sin = jnp.concatenate([-jnp.sin(ang_r), jnp.sin(ang_r), -jnp.sin(ang_c), jnp.sin(ang_c)], axis=-1)
    return cos, sin


def kernel(x, c, ctx, c_ctx, ada_w, ada_b, norm1_w, norm2_w, ret_w_in, ret_log_decay, ret_gn_w,
           ret_w_out, pool_w, pool_scale, ffn_w_in, ffn_w_out, final_norm_w):
    bsz, seq_len, _ = x.shape
    ctx_len = ctx.shape[1]
    depth = ada_w.shape[0]

    pad = (-(bsz + 1)) % 8
    c_all = jnp.concatenate([c, c_ctx[None, :], jnp.zeros((pad, D_MODEL), F32)], axis=0)
    mods = _mods(c_all, ada_w, ada_b).reshape(depth, bsz + 1 + pad, N_MOD, D_MODEL)

    rope_tabs = _rope_tables(seq_len)
    ret_w_in_b = ret_w_in.astype(BF16)
    ret_w_out_b = ret_w_out.astype(BF16)
    pool_w_b = pool_w.astype(BF16)
    ffn_w_in_b = ffn_w_in.astype(BF16)
    ffn_w_out_b = ffn_w_out.astype(BF16)

    x_lat = x.reshape(bsz * seq_len, D_MODEL)
    x_ctx = ctx.reshape(bsz * ctx_len, D_MODEL)
    for i in range(depth):
        last = i == depth - 1
        use_ret = (i % N_MIXERS) == 0
        j = i // N_MIXERS
        if last and use_ret:
            raise NotImplementedError("a final retention layer (context-state-only path) is not built")
        mod_lat = mods[i, :bsz]
        mod_ctx = mods[i, bsz:bsz + 1]
        final_w = final_norm_w if last else None
        if use_ret:
            ql, kl, vl, gl = _ret_proj(x_lat, mod_lat, norm1_w[i], ret_w_in_b, j, rope_tabs, seq_len)
            qc, kc, vc, gc = _ret_proj(x_ctx, mod_ctx, norm1_w[i], ret_w_in_b, j, None, ctx_len)
            r3 = lambda a, n: a.reshape(bsz, n, a.shape[-1])
            o_ctx, o_lat = _retention(
                ret_log_decay[j], ret_gn_w[j], ret_w_out_b, j,
                r3(qc, ctx_len), r3(kc, ctx_len), r3(vc, ctx_len), r3(gc, ctx_len),
                r3(ql, seq_len), r3(kl, seq_len), r3(vl, seq_len), r3(gl, seq_len))
            x_lat = _ret_ffn(o_lat.reshape(bsz * seq_len, D_MODEL), x_lat, mod_lat, norm2_w[i],
                             ffn_w_in_b, ffn_w_out_b, i, final_w, seq_len)
            x_ctx = _ret_ffn(o_ctx.reshape(bsz * ctx_len, D_MODEL), x_ctx, mod_ctx, norm2_w[i],
                             ffn_w_in_b, ffn_w_out_b, i, None, ctx_len)
        else:
            pool_params = (pool_w_b, j, pool_scale[j])
            if not last:
                d_ctx = _pool(x_ctx, mod_ctx, norm1_w[i], ctx_len)
                x_ctx = _mix_ffn(d_ctx, x_ctx, mod_ctx, norm2_w[i], pool_params,
                                 ffn_w_in_b, ffn_w_out_b, i, ctx_len)
            x_lat = _pool_ffn(x_lat, mod_lat, norm1_w[i], norm2_w[i], pool_params,
                              ffn_w_in_b, ffn_w_out_b, i, final_w, seq_len)
    return x_lat.reshape(bsz, seq_len, D_MODEL)
```
